```python
import jax, jax.numpy as jnp
from jax import lax
import numpy as np

D_MODEL = 2048
BATCH = 1
SEQ = 16384
DEPTH = 1
DEC_BATCH = 4
DEC_SEQ = 4096
PAST_LEN = 128

N_META = 16
GRID_W = 64
NA_HEADS = 8
NA_HEAD_DIM = 128
NA_WIN_ROWS = 8
NA_WIN_COLS = 16
SW_Q_HEADS = 8
SW_KV_HEADS = 2
SW_HEAD_DIM = 128
SW_WINDOW = 128
SW_BLOCK = 128
D_FF = 5632
CONV_W = 3
LN_EPS = 1e-5
NEG_INF = -1e30
DEEPNORM_ALPHA = (2 * DEPTH) ** 0.25
DEEPNORM_BETA = (8 * DEPTH) ** -0.25
NA_WIDTH = NA_HEADS * NA_HEAD_DIM
SW_WIDTH = SW_Q_HEADS * SW_HEAD_DIM
SW_KV_WIDTH = SW_KV_HEADS * SW_HEAD_DIM
IN_SPLITS = (NA_WIDTH, NA_WIDTH, NA_WIDTH, SW_WIDTH, SW_KV_WIDTH, SW_KV_WIDTH, D_MODEL, D_MODEL)
IN_COLS = sum(IN_SPLITS)

kernel_name = 'hybrid_natten_swa_deepnorm_encoder'


def layer_norm(x, g, b):
    xf = x.astype(jnp.float32)
    mu = jnp.mean(xf, axis=-1, keepdims=True)
    var = jnp.mean(jnp.square(xf - mu), axis=-1, keepdims=True)
    y = (xf - mu) * lax.rsqrt(var + LN_EPS) * g.astype(jnp.float32) + b.astype(jnp.float32)
    return y.astype(x.dtype)


def neighbourhood_attention(q, k, v, rpb):
    B, L, H, Dh = q.shape
    T = L - N_META
    rows = T // GRID_W
    kh = min(NA_WIN_ROWS, rows)
    scale = Dh ** -0.5
    qm, km, vm = q[:, :N_META], k[:, :N_META], v[:, :N_META]
    qg = q[:, N_META:].reshape(B, rows, GRID_W, H, Dh)
    kg = k[:, N_META:].reshape(B, rows, GRID_W, H, Dh)
    vg = v[:, N_META:].reshape(B, rows, GRID_W, H, Dh)
    r = jnp.arange(rows)
    row_start = jnp.clip(r - kh // 2, 0, rows - kh)
    key_rows = row_start[:, None] + jnp.arange(kh)[None, :]
    kb = kg[:, key_rows]
    vb = vg[:, key_rows]
    c = jnp.arange(GRID_W)
    col_start = jnp.clip(c - NA_WIN_COLS // 2, 0, GRID_W - NA_WIN_COLS)
    col_in = (c[None, :] >= col_start[:, None]) & (c[None, :] < col_start[:, None] + NA_WIN_COLS)
    dr = key_rows - r[:, None] + (NA_WIN_ROWS - 1)
    dc = jnp.clip(c[None, :] - c[:, None], -(NA_WIN_COLS - 1), NA_WIN_COLS - 1) + (NA_WIN_COLS - 1)
    bias = rpb.astype(jnp.float32)[:, dr[:, None, :, None], dc[None, :, None, :]]
    bias = jnp.moveaxis(bias, 0, 1)
    bias = jnp.where(col_in[None, None, :, None, :], bias, NEG_INF)
    s_loc = jnp.einsum('brqhd,brkwhd->brhqkw', qg, kb, preferred_element_type=jnp.float32) * scale + bias[None]
    s_loc = s_loc.reshape(B, rows, H, GRID_W, kh * GRID_W)
    s_met = jnp.einsum('brqhd,bmhd->brhqm', qg, km, preferred_element_type=jnp.float32) * scale
    p = jax.nn.softmax(jnp.concatenate([s_loc, s_met], axis=-1), axis=-1).astype(v.dtype)
    p_loc = p[..., :kh * GRID_W].reshape(B, rows, H, GRID_W, kh, GRID_W)
    p_met = p[..., kh * GRID_W:]
    out = jnp.einsum('brhqkw,brkwhd->brqhd', p_loc, vb) + jnp.einsum('brhqm,bmhd->brqhd', p_met, vm)
    out = out.reshape(B, T, H, Dh)
    s_mm = jnp.einsum('bqhd,bmhd->bhqm', qm, km, preferred_element_type=jnp.float32) * scale
    p_mm = jax.nn.softmax(s_mm, axis=-1).astype(v.dtype)
    out_m = jnp.einsum('bhqm,bmhd->bqhd', p_mm, vm)
    return jnp.concatenate([out_m, out], axis=1)


def sliding_window_attention(q, k, v, sink):
    B, L, HQ, Dh = q.shape
    G = k.shape[2]
    R = HQ // G
    T = L - N_META
    nb = T // SW_BLOCK
    scale = Dh ** -0.5
    slopes = jnp.power(2.0, -8.0 * jnp.arange(1, HQ + 1, dtype=jnp.float32) / HQ).reshape(G, R)
    sink = sink.astype(jnp.float32).reshape(G, R)
    qm = q[:, :N_META].reshape(B, N_META, G, R, Dh)
    km, vm = k[:, :N_META], v[:, :N_META]
    qr = q[:, N_META:].reshape(B, nb, SW_BLOCK, G, R, Dh)
    kr = k[:, N_META:].reshape(B, nb, SW_BLOCK, G, Dh)
    vr = v[:, N_META:].reshape(B, nb, SW_BLOCK, G, Dh)
    pad = ((0, 0), (1, 1), (0, 0), (0, 0), (0, 0))
    kp, vp = jnp.pad(kr, pad), jnp.pad(vr, pad)
    kband = jnp.concatenate([kp[:, :-2], kp[:, 1:-1], kp[:, 2:]], axis=2)
    vband = jnp.concatenate([vp[:, :-2], vp[:, 1:-1], vp[:, 2:]], axis=2)
    qpos = jnp.arange(nb)[:, None] * SW_BLOCK + jnp.arange(SW_BLOCK)[None, :]
    kpos = (jnp.arange(nb)[:, None] - 1) * SW_BLOCK + jnp.arange(3 * SW_BLOCK)[None, :]
    dist = jnp.abs(qpos[:, :, None] - kpos[:, None, :])
    valid = (dist <= SW_WINDOW) & (kpos[:, None, :] >= 0) & (kpos[:, None, :] < T)
    penalty = dist[:, None, None] * slopes[None, :, :, None, None]
    s_loc = jnp.einsum('bnqgrd,bnkgd->bngrqk', qr, kband, preferred_element_type=jnp.float32) * scale
    s_loc = jnp.where(valid[:, None, None], s_loc - penalty, NEG_INF)
    s_met = jnp.einsum('bnqgrd,bmgd->bngrqm', qr, km, preferred_element_type=jnp.float32) * scale
    s_snk = jnp.broadcast_to(sink[None, None, :, :, None, None], s_loc.shape[:-1] + (1,))
    p = jax.nn.softmax(jnp.concatenate([s_loc, s_met, s_snk], axis=-1), axis=-1).astype(v.dtype)
    nk = 3 * SW_BLOCK
    out = (jnp.einsum('bngrqk,bnkgd->bnqgrd', p[..., :nk], vband)
           + jnp.einsum('bngrqm,bmgd->bnqgrd', p[..., nk:nk + N_META], vm))
    out = out.reshape(B, T, HQ, Dh)
    dist_m = (N_META + jnp.arange(SW_BLOCK))[None, :] - jnp.arange(N_META)[:, None]
    s_mm = jnp.einsum('bqgrd,bmgd->bgrqm', qm, km, preferred_element_type=jnp.float32) * scale
    s_mr = jnp.einsum('bqgrd,bkgd->bgrqk', qm, kr[:, 0], preferred_element_type=jnp.float32) * scale
    s_mr = jnp.where(dist_m <= SW_WINDOW, s_mr - slopes[:, :, None, None] * dist_m, NEG_INF)
    s_ms = jnp.broadcast_to(sink[None, :, :, None, None], s_mm.shape[:-1] + (1,))
    p_m = jax.nn.softmax(jnp.concatenate([s_mm, s_mr, s_ms], axis=-1), axis=-1).astype(v.dtype)
    out_m = (jnp.einsum('bgrqm,bmgd->bqgrd', p_m[..., :N_META], vm)
             + jnp.einsum('bgrqk,bkgd->bqgrd', p_m[..., N_META:N_META + SW_BLOCK], vr[:, 0]))
    out_m = out_m.reshape(B, N_META, HQ, Dh)
    return jnp.concatenate([out_m, out], axis=1)


def depthwise_conv_centred(u, w, b):
    c = u.shape[-1]
    y = lax.conv_general_dilated(u, w[:, None, :].astype(u.dtype), window_strides=(1,),
                                 padding=((CONV_W // 2, CONV_W // 2),),
                                 dimension_numbers=('NWC', 'WIO', 'NWC'), feature_group_count=c)
    return y + b.astype(u.dtype)


def encoder_layer(x, w_in, na_rpb, sw_sink, w_proj_na, w_proj_sw, w_out, ln1_g, ln1_b,
                  w_ffn_in, ffn_conv_w, ffn_conv_b, w_ffn_down, ln2_g, ln2_b):
    B, L, _ = x.shape
    proj = x @ w_in
    qa, ka, va, qb, kb, vb, ga, gb = jnp.split(proj, np.cumsum(IN_SPLITS)[:-1].tolist(), axis=-1)
    qa = qa.reshape(B, L, NA_HEADS, NA_HEAD_DIM)
    ka = ka.reshape(B, L, NA_HEADS, NA_HEAD_DIM)
    va = va.reshape(B, L, NA_HEADS, NA_HEAD_DIM)
    qb = qb.reshape(B, L, SW_Q_HEADS, SW_HEAD_DIM)
    kb = kb.reshape(B, L, SW_KV_HEADS, SW_HEAD_DIM)
    vb = vb.reshape(B, L, SW_KV_HEADS, SW_HEAD_DIM)
    oa = neighbourhood_attention(qa, ka, va, na_rpb).reshape(B, L, NA_WIDTH)
    ob = sliding_window_attention(qb, kb, vb, sw_sink).reshape(B, L, SW_WIDTH)
    merged = jax.nn.sigmoid(ga) * (oa @ w_proj_na) + jax.nn.sigmoid(gb) * (ob @ w_proj_sw)
    h = layer_norm(DEEPNORM_ALPHA * x + merged @ w_out, ln1_g, ln1_b)
    gate, val = jnp.split(h @ w_ffn_in, 2, axis=-1)
    gate = depthwise_conv_centred(gate, ffn_conv_w, ffn_conv_b)
    f = (jax.nn.gelu(gate) * val) @ w_ffn_down
    return layer_norm(DEEPNORM_ALPHA * h + f, ln2_g, ln2_b)


def encoder_trunk(x, meta_tokens, ln_emb_g, ln_emb_b, w_in, na_rpb, sw_sink, w_proj_na, w_proj_sw, w_out,
                  ln1_g, ln1_b, w_ffn_in, ffn_conv_w, ffn_conv_b, w_ffn_down, ln2_g, ln2_b):
    B = x.shape[0]
    meta = jnp.broadcast_to(meta_tokens.astype(x.dtype)[None], (B, N_META, D_MODEL))
    h = layer_norm(jnp.concatenate([meta, x], axis=1), ln_emb_g, ln_emb_b)
    for i in range(DEPTH):
        h = encoder_layer(h, w_in[i], na_rpb[i], sw_sink[i], w_proj_na[i], w_proj_sw[i], w_out[i],
                          ln1_g[i], ln1_b[i], w_ffn_in[i], ffn_conv_w[i], ffn_conv_b[i], w_ffn_down[i],
                          ln2_g[i], ln2_b[i])
    return h[:, N_META:]


def setup_inputs(seed: int = 0) -> dict:
    key = jax.random.key(seed)
    ks = jax.random.split(key, 20)
    nrm = jax.random.normal
    f32 = jnp.float32
    return {
        'x_prompt': nrm(ks[0], (BATCH, SEQ, D_MODEL), f32),
        'x_sample': nrm(ks[1], (DEC_BATCH, DEC_SEQ, D_MODEL), f32),
        'meta_tokens': nrm(ks[2], (N_META, D_MODEL), f32),
        'ln_emb_g': 1.0 + 0.01 * nrm(ks[3], (D_MODEL,), f32),
        'ln_emb_b': 0.01 * nrm(ks[4], (D_MODEL,), f32),
        'w_in': nrm(ks[5], (DEPTH, D_MODEL, IN_COLS), f32) * D_MODEL ** -0.5,
        'na_rpb': 0.1 * nrm(ks[6], (DEPTH, NA_HEADS, 2 * NA_WIN_ROWS - 1, 2 * NA_WIN_COLS - 1), f32),
        'sw_sink': 0.5 * nrm(ks[7], (DEPTH, SW_Q_HEADS), f32),
        'w_proj_na': nrm(ks[8], (DEPTH, NA_WIDTH, D_MODEL), f32) * NA_WIDTH ** -0.5,
        'w_proj_sw': nrm(ks[9], (DEPTH, SW_WIDTH, D_MODEL), f32) * SW_WIDTH ** -0.5,
        'w_out': nrm(ks[10], (DEPTH, D_MODEL, D_MODEL), f32) * (D_MODEL ** -0.5 * DEEPNORM_BETA),
        'ln1_g': 1.0 + 0.01 * nrm(ks[11], (DEPTH, D_MODEL), f32),
        'ln1_b': 0.01 * nrm(ks[12], (DEPTH, D_MODEL), f32),
        'w_ffn_in': nrm(ks[13], (DEPTH, D_MODEL, 2 * D_FF), f32) * D_MODEL ** -0.5,
        'ffn_conv_w': nrm(ks[14], (DEPTH, CONV_W, D_FF), f32) * CONV_W ** -0.5,
        'ffn_conv_b': 0.01 * nrm(ks[15], (DEPTH, D_FF), f32),
        'w_ffn_down': nrm(ks[16], (DEPTH, D_FF, D_MODEL), f32) * (D_FF ** -0.5 * DEEPNORM_BETA),
        'ln2_g': 1.0 + 0.01 * nrm(ks[17], (DEPTH, D_MODEL), f32),
        'ln2_b': 0.01 * nrm(ks[18], (DEPTH, D_MODEL), f32),
    }


def reference(x_prompt, x_sample, meta_tokens, ln_emb_g, ln_emb_b, w_in, na_rpb, sw_sink, w_proj_na, w_proj_sw,
              w_out, ln1_g, ln1_b, w_ffn_in, ffn_conv_w, ffn_conv_b, w_ffn_down, ln2_g, ln2_b):
    y_prompt = encoder_trunk(x_prompt, meta_tokens, ln_emb_g, ln_emb_b, w_in, na_rpb, sw_sink, w_proj_na,
                             w_proj_sw, w_out, ln1_g, ln1_b, w_ffn_in, ffn_conv_w, ffn_conv_b, w_ffn_down,
                             ln2_g, ln2_b)
    y_sample = encoder_trunk(x_sample, meta_tokens, ln_emb_g, ln_emb_b, w_in, na_rpb, sw_sink, w_proj_na,
                             w_proj_sw, w_out, ln1_g, ln1_b, w_ffn_in, ffn_conv_w, ffn_conv_b, w_ffn_down,
                             ln2_g, ln2_b)
    return (y_prompt, y_sample)
```

```python
import functools

import numpy as np
import jax
import jax.numpy as jnp
from jax import lax
from jax.experimental import pallas as pl
from jax.experimental.pallas import tpu as pltpu

D_MODEL = 2048
N_META = 16
GRID_W = 64
NA_HEADS = 8
HEAD_DIM = 128
NA_WIN_ROWS = 8
NA_WIN_COLS = 16
SW_Q_HEADS = 8
SW_KV_HEADS = 2
SW_GROUP = SW_Q_HEADS // SW_KV_HEADS
SW_WINDOW = 128
SW_BLOCK = 128
D_FF = 5632
LN_EPS = 1e-5
NEG_INF = -1e30
DEPTH = 1
ALPHA = (2 * DEPTH) ** 0.25
SCALE = HEAD_DIM ** -0.5

LANE = 128
BF16_ROWS = 16
VMEM_LIMIT = 56 * 1024 * 1024

SLAB_GA, SLAB_GB, SLAB_QA, SLAB_KA, SLAB_VA, SLAB_QB, SLAB_KB, SLAB_VB = 0, 16, 32, 40, 48, 56, 64, 66
N_SLABS = 68
ORIG_QKV_COLS = 4608

NA_QROWS = 4
NA_WROWS = 11
NA_QBLK = NA_QROWS * GRID_W
NA_WKEYS = NA_WROWS * GRID_W
NA_KEYS = 768
SW_WKEYS = 3 * SW_BLOCK
SW_KEYS = 512
SW_MKEYS = 256

F32 = jnp.float32
BF16 = jnp.bfloat16


def _ln_rows(x, g, b):
    mu = jnp.mean(x, axis=-1, keepdims=True)
    xc = x - mu
    var = jnp.mean(xc * xc, axis=-1, keepdims=True)
    return xc * lax.rsqrt(var + LN_EPS) * g + b


def _softmax_pv(q, kcat, vcat, bias):
    s = lax.dot_general(q, kcat, (((1,), (1,)), ((), ())), preferred_element_type=F32)
    s = s * SCALE
    if bias is not None:
        s = s + bias
    m = jnp.max(s, axis=-1, keepdims=True)
    e = jnp.exp(s - m)
    l = jnp.sum(e, axis=-1, keepdims=True)
    o = jnp.dot(e.astype(BF16), vcat, preferred_element_type=F32)
    return o / l


def _inproj_kernel(x_ref, g_ref, b_ref, w_ref, o_ref, xn_ref, *, chunks):
    @pl.when(pl.program_id(1) == 0)
    def _():
        xn_ref[...] = _ln_rows(x_ref[...], g_ref[...], b_ref[...]).astype(BF16)

    xn = xn_ref[...]
    for c0, cw in chunks:
        r = jnp.dot(xn, w_ref[:, c0:c0 + cw], preferred_element_type=F32)
        for s in range(cw // LANE):
            o_ref[c0 // LANE + s] = r[:, s * LANE:(s + 1) * LANE].astype(BF16)


def _inproj(x2, g, b, w, tm):
    m = x2.shape[0]
    n_tiles = 4
    tn = w.shape[1] // n_tiles
    slabs = tn // LANE
    chunks = tuple((c, min(512, tn - c)) for c in range(0, tn, 512))
    return pl.pallas_call(
        functools.partial(_inproj_kernel, chunks=chunks),
        out_shape=jax.ShapeDtypeStruct((N_SLABS, m, LANE), BF16),
        grid=(m // tm, n_tiles),
        in_specs=[
            pl.BlockSpec((tm, D_MODEL), lambda i, j: (i, 0)),
            pl.BlockSpec((1, D_MODEL), lambda i, j: (0, 0)),
            pl.BlockSpec((1, D_MODEL), lambda i, j: (0, 0)),
            pl.BlockSpec((D_MODEL, tn), lambda i, j: (0, j)),
        ],
        out_specs=pl.BlockSpec((slabs, tm, LANE), lambda i, j: (j, i, 0)),
        scratch_shapes=[pltpu.VMEM((tm, D_MODEL), BF16)],
        compiler_params=pltpu.CompilerParams(
            dimension_semantics=("arbitrary", "arbitrary"), vmem_limit_bytes=VMEM_LIMIT),
        name="inproj",
    )(x2, g, b, w)


def _na_kernel(q_ref, k_ref, v_ref, km_ref, vm_ref, bias_ref, o_ref, kcat_ref, vcat_ref, *, rows, nblk):
    pad = jnp.zeros((NA_KEYS - NA_WKEYS - N_META, HEAD_DIM), BF16)
    kcat_ref[NA_WKEYS:NA_WKEYS + N_META, :] = km_ref[0]
    vcat_ref[NA_WKEYS:NA_WKEYS + N_META, :] = vm_ref[0]
    kcat_ref[NA_WKEYS + N_META:, :] = pad
    vcat_ref[NA_WKEYS + N_META:, :] = pad

    def body(i, carry):
        r0 = i * NA_QROWS
        w0 = jnp.minimum(jnp.clip(r0 - NA_WIN_ROWS // 2, 0, rows - NA_WIN_ROWS), rows - NA_WROWS)
        start = pl.multiple_of(w0 * GRID_W, GRID_W)
        kcat_ref[0:NA_WKEYS, :] = k_ref[0, pl.ds(start, NA_WKEYS), :]
        vcat_ref[0:NA_WKEYS, :] = v_ref[0, pl.ds(start, NA_WKEYS), :]
        variant = jnp.where(i == 0, 0, jnp.where(i == nblk - 1, 2, 1))
        qs = pl.multiple_of(i * NA_QBLK, NA_QBLK)
        q = q_ref[0, pl.ds(qs, NA_QBLK), :]
        o = _softmax_pv(q, kcat_ref[...], vcat_ref[...], bias_ref[variant, 0])
        o_ref[0, pl.ds(qs, NA_QBLK), :] = o.astype(BF16)
        return carry

    lax.fori_loop(0, nblk, body, 0)


def _na_attn(p, pm, bias, b_sz, t):
    rows = t // GRID_W
    nblk = rows // NA_QROWS
    assert rows % NA_QROWS == 0 and nblk >= 3 and rows >= NA_WROWS
    seq = lambda slab: pl.BlockSpec((1, t, HEAD_DIM), lambda b, h: (slab + h, b, 0))
    meta = lambda slab: pl.BlockSpec((1, N_META, HEAD_DIM), lambda b, h: (slab + h, 0, 0))
    return pl.pallas_call(
        functools.partial(_na_kernel, rows=rows, nblk=nblk),
        out_shape=jax.ShapeDtypeStruct((NA_HEADS, b_sz * t, HEAD_DIM), BF16),
        grid=(b_sz, NA_HEADS),
        in_specs=[seq(SLAB_QA), seq(SLAB_KA), seq(SLAB_VA), meta(SLAB_KA), meta(SLAB_VA),
                  pl.BlockSpec((3, 1, NA_QBLK, NA_KEYS), lambda b, h: (0, h, 0, 0))],
        out_specs=pl.BlockSpec((1, t, HEAD_DIM), lambda b, h: (h, b, 0)),
        scratch_shapes=[pltpu.VMEM((NA_KEYS, HEAD_DIM), BF16), pltpu.VMEM((NA_KEYS, HEAD_DIM), BF16)],
        compiler_params=pltpu.CompilerParams(
            dimension_semantics=("arbitrary", "arbitrary"), vmem_limit_bytes=VMEM_LIMIT),
        name="na_attn",
    )(p, p, p, pm, pm, bias)


def _na_bias(rpb, dtype=F32):
    qi = np.arange(NA_QROWS)[:, None, None, None]
    qc = np.arange(GRID_W)[None, :, None, None]
    j = np.arange(NA_WROWS)[None, None, :, None]
    kc = np.arange(GRID_W)[None, None, None, :]
    col_start = np.clip(qc - NA_WIN_COLS // 2, 0, GRID_W - NA_WIN_COLS)
    in_cols = (kc >= col_start) & (kc < col_start + NA_WIN_COLS)
    dc = np.clip(kc - qc, -(NA_WIN_COLS - 1), NA_WIN_COLS - 1) + (NA_WIN_COLS - 1)
    full = (NA_QROWS, GRID_W, NA_WROWS, GRID_W)
    tables = []
    for qoff, rs in ((qi, 0 * qi), (qi + NA_WIN_ROWS // 2, qi), (qi + NA_WROWS - NA_QROWS, 0 * qi + NA_WROWS - NA_WIN_ROWS)):
        in_rows = (j >= rs) & (j < rs + NA_WIN_ROWS)
        dr = np.clip(j - qoff + (NA_WIN_ROWS - 1), 0, 2 * NA_WIN_ROWS - 2)
        mask = np.broadcast_to(in_rows & in_cols, full)
        tab = rpb.astype(dtype)[:, np.broadcast_to(dr, full), np.broadcast_to(dc, full)]
        tab = jnp.where(mask[None], tab, NEG_INF).reshape(NA_HEADS, NA_QBLK, NA_WKEYS)
        tables.append(tab)
    tab = jnp.stack(tables)
    lead = tab.shape[:-1]
    return jnp.concatenate([tab, jnp.zeros(lead + (N_META,), dtype),
                            jnp.full(lead + (NA_KEYS - NA_WKEYS - N_META,), NEG_INF, dtype)], axis=-1)


def _swa_kernel(q_ref, k_ref, v_ref, km_ref, vm_ref, bias_ref, o_ref, kcat_ref, vcat_ref, *, t, nb, bps):
    pad = jnp.zeros((SW_KEYS - SW_WKEYS - N_META, HEAD_DIM), BF16)
    kcat_ref[SW_WKEYS:SW_WKEYS + N_META, :] = km_ref[0]
    vcat_ref[SW_WKEYS:SW_WKEYS + N_META, :] = vm_ref[0]
    kcat_ref[SW_WKEYS + N_META:, :] = pad
    vcat_ref[SW_WKEYS + N_META:, :] = pad
    step = pl.program_id(2)

    def body(i, carry):
        n = step * bps + i
        start = pl.multiple_of(jnp.clip((n - 1) * SW_BLOCK, 0, t - SW_WKEYS), SW_BLOCK)
        kcat_ref[0:SW_WKEYS, :] = k_ref[0, pl.ds(start, SW_WKEYS), :]
        vcat_ref[0:SW_WKEYS, :] = v_ref[0, pl.ds(start, SW_WKEYS), :]
        variant = jnp.where(n == 0, 0, jnp.where(n == nb - 1, 2, 1))
        qs = pl.multiple_of(i * SW_BLOCK, SW_BLOCK)
        q = q_ref[:, pl.ds(qs, SW_BLOCK), :].reshape(SW_GROUP * SW_BLOCK, HEAD_DIM)
        bias = bias_ref[variant].reshape(SW_GROUP * SW_BLOCK, SW_KEYS)
        o = _softmax_pv(q, kcat_ref[...], vcat_ref[...], bias)
        o_ref[:, pl.ds(qs, SW_BLOCK), :] = o.reshape(SW_GROUP, SW_BLOCK, HEAD_DIM).astype(BF16)
        return carry

    lax.fori_loop(0, bps, body, 0)


def _swa_attn(p, pm, bias, b_sz, t, tq):
    nb = t // SW_BLOCK
    assert nb >= 3 and t % tq == 0
    steps = t // tq
    bps = tq // SW_BLOCK
    kv = lambda slab: pl.BlockSpec((1, t, HEAD_DIM), lambda b, g, s: (slab + g, b, 0))
    meta = lambda slab: pl.BlockSpec((1, N_META, HEAD_DIM), lambda b, g, s: (slab + g, 0, 0))
    return pl.pallas_call(
        functools.partial(_swa_kernel, t=t, nb=nb, bps=bps),
        out_shape=jax.ShapeDtypeStruct((SW_Q_HEADS, b_sz * t, HEAD_DIM), BF16),
        grid=(b_sz, SW_KV_HEADS, steps),
        in_specs=[pl.BlockSpec((SW_GROUP, tq, HEAD_DIM), lambda b, g, s: (SLAB_QB // SW_GROUP + g, b * steps + s, 0)),
                  kv(SLAB_KB), kv(SLAB_VB), meta(SLAB_KB), meta(SLAB_VB),
                  pl.BlockSpec((3, SW_GROUP, SW_BLOCK, SW_KEYS), lambda b, g, s: (0, g, 0, 0))],
        out_specs=pl.BlockSpec((SW_GROUP, tq, HEAD_DIM), lambda b, g, s: (g, b * steps + s, 0)),
        scratch_shapes=[pltpu.VMEM((SW_KEYS, HEAD_DIM), BF16), pltpu.VMEM((SW_KEYS, HEAD_DIM), BF16)],
        compiler_params=pltpu.CompilerParams(
            dimension_semantics=("arbitrary", "arbitrary", "arbitrary"), vmem_limit_bytes=VMEM_LIMIT),
        name="swa_attn",
    )(p, p, p, pm, pm, bias)


def _sw_slopes():
    return np.power(2.0, -8.0 * np.arange(1, SW_Q_HEADS + 1, dtype=np.float64) / SW_Q_HEADS).astype(np.float32)


def _sw_bias(sink, dtype=F32):
    i = np.arange(SW_BLOCK)[:, None]
    j = np.arange(SW_WKEYS)[None, :]
    slopes = jnp.asarray(_sw_slopes(), dtype)[:, None, None]
    tables = []
    for qo in (0, SW_BLOCK, 2 * SW_BLOCK):
        dist = np.abs(qo + i - j)
        pen = -(jnp.asarray(dist, dtype)[None] * slopes)
        tables.append(jnp.where((dist <= SW_WINDOW)[None], pen, NEG_INF))
    tab = jnp.stack(tables)
    lead = tab.shape[:-1]
    snk = jnp.broadcast_to(sink.astype(dtype)[None, :, None, None], lead + (1,))
    return jnp.concatenate([tab, jnp.zeros(lead + (N_META,), dtype), snk,
                            jnp.full(lead + (SW_KEYS - SW_WKEYS - N_META - 1,), NEG_INF, dtype)], axis=-1)


def _sw_meta_bias(sink, dtype=F32):
    q = np.arange(N_META)[:, None]
    k = np.arange(SW_BLOCK)[None, :]
    dist = (N_META + k) - q
    slopes = jnp.asarray(_sw_slopes(), dtype)[:, None, None]
    pen = jnp.where((dist <= SW_WINDOW)[None], -(slopes * jnp.asarray(dist, dtype)[None]), NEG_INF)
    lead = pen.shape[:-1]
    snk = jnp.broadcast_to(sink.astype(dtype)[:, None, None], lead + (1,))
    return jnp.concatenate([jnp.zeros(lead + (N_META,), dtype), pen, snk,
                            jnp.full(lead + (SW_MKEYS - N_META - SW_BLOCK - 1,), NEG_INF, dtype)], axis=-1)


def _meta_attn_kernel(pm_ref, k0_ref, v0_ref, bias_ref, oa_ref, ob_ref, kcat_ref, vcat_ref):
    for h in range(NA_HEADS):
        o = _softmax_pv(pm_ref[SLAB_QA + h], pm_ref[SLAB_KA + h], pm_ref[SLAB_VA + h], None)
        oa_ref[h] = o.astype(BF16)
    pad = jnp.zeros((SW_MKEYS - N_META - SW_BLOCK, HEAD_DIM), BF16)
    for g in range(SW_KV_HEADS):
        kcat_ref[0:N_META, :] = pm_ref[SLAB_KB + g]
        vcat_ref[0:N_META, :] = pm_ref[SLAB_VB + g]
        kcat_ref[N_META:N_META + SW_BLOCK, :] = k0_ref[g]
        vcat_ref[N_META:N_META + SW_BLOCK, :] = v0_ref[g]
        kcat_ref[N_META + SW_BLOCK:, :] = pad
        vcat_ref[N_META + SW_BLOCK:, :] = pad
        q = jnp.concatenate([pm_ref[SLAB_QB + g * SW_GROUP + r] for r in range(SW_GROUP)], axis=0)
        bias = bias_ref[g * SW_GROUP:(g + 1) * SW_GROUP].reshape(SW_GROUP * N_META, SW_MKEYS)
        o = _softmax_pv(q, kcat_ref[...], vcat_ref[...], bias)
        for r in range(SW_GROUP):
            ob_ref[g * SW_GROUP + r] = o[r * N_META:(r + 1) * N_META].astype(BF16)


def _meta_attn(p, pm, bias, b_sz, t):
    blocks_per_seq = t // SW_BLOCK
    first_block = lambda slab: pl.BlockSpec((SW_KV_HEADS, SW_BLOCK, HEAD_DIM),
                                            lambda b: (slab // SW_KV_HEADS, b * blocks_per_seq, 0))
    out = jax.ShapeDtypeStruct((NA_HEADS, b_sz * N_META, HEAD_DIM), BF16)
    return pl.pallas_call(
        _meta_attn_kernel,
        out_shape=(out, out),
        grid=(b_sz,),
        in_specs=[pl.BlockSpec((N_SLABS, N_META, HEAD_DIM), lambda b: (0, 0, 0)),
                  first_block(SLAB_KB), first_block(SLAB_VB),
                  pl.BlockSpec((SW_Q_HEADS, N_META, SW_MKEYS), lambda b: (0, 0, 0))],
        out_specs=(pl.BlockSpec((NA_HEADS, N_META, HEAD_DIM), lambda b: (0, b, 0)),
                   pl.BlockSpec((SW_Q_HEADS, N_META, HEAD_DIM), lambda b: (0, b, 0))),
        scratch_shapes=[pltpu.VMEM((SW_MKEYS, HEAD_DIM), BF16), pltpu.VMEM((SW_MKEYS, HEAD_DIM), BF16)],
        compiler_params=pltpu.CompilerParams(dimension_semantics=("arbitrary",)),
        name="meta_attn",
    )(pm, p, p, bias)


def _post_kernel(x_ref, lg_ref, lb_ref, oa_ref, ob_ref, ga_ref, gb_ref, wna_ref, wsw_ref, wout_ref,
                 g1_ref, b1_ref, h_ref):
    h0 = _ln_rows(x_ref[...], lg_ref[...], lb_ref[...])
    oa = jnp.concatenate([oa_ref[h] for h in range(NA_HEADS)], axis=1)
    ob = jnp.concatenate([ob_ref[h] for h in range(SW_Q_HEADS)], axis=1)
    a = jnp.dot(oa, wna_ref[...], preferred_element_type=F32)
    b = jnp.dot(ob, wsw_ref[...], preferred_element_type=F32)
    n_g = D_MODEL // LANE
    ga = jnp.concatenate([ga_ref[c] for c in range(n_g)], axis=1).astype(F32)
    gb = jnp.concatenate([gb_ref[c] for c in range(n_g)], axis=1).astype(F32)
    merged = jax.nn.sigmoid(ga) * a + jax.nn.sigmoid(gb) * b
    y = jnp.dot(merged.astype(BF16), wout_ref[...], preferred_element_type=F32)
    h_ref[...] = _ln_rows(ALPHA * h0 + y, g1_ref[...], b1_ref[...])


def _post_attn(x2, lg, lb, oa, ob, gates, wna, wsw, wout, g1, b1, tm):
    m = x2.shape[0]
    n_g = D_MODEL // LANE
    const = lambda shape: pl.BlockSpec(shape, lambda i: (0,) * len(shape), pipeline_mode=pl.Buffered(1))
    heads = pl.BlockSpec((NA_HEADS, tm, HEAD_DIM), lambda i: (0, i, 0))
    return pl.pallas_call(
        _post_kernel,
        out_shape=jax.ShapeDtypeStruct((m, D_MODEL), F32),
        grid=(m // tm,),
        in_specs=[pl.BlockSpec((tm, D_MODEL), lambda i: (i, 0)),
                  const((1, D_MODEL)), const((1, D_MODEL)),
                  heads, heads,
                  pl.BlockSpec((n_g, tm, LANE), lambda i: (SLAB_GA // n_g, i, 0)),
                  pl.BlockSpec((n_g, tm, LANE), lambda i: (SLAB_GB // n_g, i, 0)),
                  const(wna.shape), const(wsw.shape), const(wout.shape),
                  const((1, D_MODEL)), const((1, D_MODEL))],
        out_specs=pl.BlockSpec((tm, D_MODEL), lambda i: (i, 0)),
        compiler_params=pltpu.CompilerParams(dimension_semantics=("arbitrary",), vmem_limit_bytes=VMEM_LIMIT),
        name="post_attn",
    )(x2, lg, lb, oa, ob, gates, gates, wna, wsw, wout, g1, b1)


HALO = BF16_ROWS


def _ffn_kernel(h_ref, hp_ref, hn_ref, hm_ref, wg_ref, wv_ref, cw_ref, cb_ref, wd_ref, g2_ref, b2_ref,
                o_ref, hb_ref, *, tm, tiles_per_seq):
    i = pl.program_id(0)
    f = pl.program_id(1)
    nf = pl.num_programs(1)

    @pl.when(f == 0)
    def _():
        pos = i % tiles_per_seq
        prev = jnp.where(pos == 0, hm_ref[...], hp_ref[...])
        nxt = jnp.where(pos == tiles_per_seq - 1, 0.0, hn_ref[...])
        hb_ref[0:HALO, :] = prev.astype(BF16)
        hb_ref[HALO:HALO + tm, :] = h_ref[...].astype(BF16)
        hb_ref[HALO + tm:, :] = nxt.astype(BF16)

    n = tm + 2 * HALO
    gp = jnp.dot(hb_ref[...], wg_ref[...], preferred_element_type=F32)
    vl = jnp.dot(hb_ref[HALO:HALO + tm, :], wv_ref[...], preferred_element_type=F32)
    up = pltpu.roll(gp, 1, 0)
    dn = pltpu.roll(gp, n - 1, 0)
    cw = cw_ref[...]
    gate = (up * cw[0:1, :] + gp * cw[1:2, :] + dn * cw[2:3, :])[HALO:HALO + tm] + cb_ref[...]
    act = (jax.nn.gelu(gate) * vl).astype(BF16)
    part = jnp.dot(act, wd_ref[...], preferred_element_type=F32)

    @pl.when(f == 0)
    def _():
        o_ref[...] = part

    @pl.when(f > 0)
    def _():
        o_ref[...] += part

    @pl.when(f == nf - 1)
    def _():
        o_ref[...] = _ln_rows(ALPHA * h_ref[...] + o_ref[...], g2_ref[...], b2_ref[...])


def _ffn(h, h_meta, w_in, cw, cb, wd, g2, b2, t, tm, tf):
    m = h.shape[0]
    tiles_per_seq = t // tm
    nf = D_FF // tf
    hb = tm // HALO
    last_hb = m // HALO - 1
    return pl.pallas_call(
        functools.partial(_ffn_kernel, tm=tm, tiles_per_seq=tiles_per_seq),
        out_shape=jax.ShapeDtypeStruct((m, D_MODEL), F32),
        grid=(m // tm, nf),
        in_specs=[pl.BlockSpec((tm, D_MODEL), lambda i, f: (i, 0)),
                  pl.BlockSpec((HALO, D_MODEL), lambda i, f: (jnp.maximum(i * hb - 1, 0), 0)),
                  pl.BlockSpec((HALO, D_MODEL), lambda i, f: (jnp.minimum((i + 1) * hb, last_hb), 0)),
                  pl.BlockSpec((N_META, D_MODEL), lambda i, f: (i // tiles_per_seq, 0)),
                  pl.BlockSpec((D_MODEL, tf), lambda i, f: (0, f)),
                  pl.BlockSpec((D_MODEL, tf), lambda i, f: (0, f + nf)),
                  pl.BlockSpec((3, tf), lambda i, f: (0, f)),
                  pl.BlockSpec((1, tf), lambda i, f: (0, f)),
                  pl.BlockSpec((tf, D_MODEL), lambda i, f: (f, 0)),
                  pl.BlockSpec((1, D_MODEL), lambda i, f: (0, 0)),
                  pl.BlockSpec((1, D_MODEL), lambda i, f: (0, 0))],
        out_specs=pl.BlockSpec((tm, D_MODEL), lambda i, f: (i, 0)),
        scratch_shapes=[pltpu.VMEM((tm + 2 * HALO, D_MODEL), BF16)],
        compiler_params=pltpu.CompilerParams(
            dimension_semantics=("arbitrary", "arbitrary"), vmem_limit_bytes=VMEM_LIMIT),
        name="ffn",
    )(h, h, h, h_meta, w_in, w_in, cw, cb, wd, g2, b2)


def _layer_weights(w_in, w_proj_na, w_proj_sw, w_out, w_ffn_in, w_ffn_down):
    w_in_p = jnp.concatenate([w_in[:, ORIG_QKV_COLS:], w_in[:, :ORIG_QKV_COLS]], axis=1).astype(BF16)
    return (w_in_p, w_proj_na.astype(BF16), w_proj_sw.astype(BF16), w_out.astype(BF16),
            w_ffn_in.astype(BF16), w_ffn_down.astype(BF16))


def _trunk(x, meta_tokens, pm, lg, lb, wts, tables, g1, b1, cw, cb, g2, b2):
    b_sz, t, _ = x.shape
    w_in_p, wna, wsw, wout, wffn, wdown = wts
    na_bias, sw_bias, swm_bias = tables
    x2 = x.reshape(b_sz * t, D_MODEL)
    p = _inproj(x2, lg, lb, w_in_p, tm=512)
    oa = _na_attn(p, pm, na_bias, b_sz, t)
    ob = _swa_attn(p, pm, sw_bias, b_sz, t, tq=2048)
    oa_m, ob_m = _meta_attn(p, pm, swm_bias, b_sz, t)
    h = _post_attn(x2, lg, lb, oa, ob, p, wna, wsw, wout, g1, b1, tm=256)
    x_m = jnp.tile(meta_tokens, (b_sz, 1))
    gates_m = jnp.tile(pm[:SLAB_QA], (1, b_sz, 1))
    h_m = _post_attn(x_m, lg, lb, oa_m, ob_m, gates_m, wna, wsw, wout, g1, b1, tm=b_sz * N_META)
    y = _ffn(h, h_m, wffn, cw, cb, wdown, g2, b2, t, tm=512, tf=512)
    return y.reshape(b_sz, t, D_MODEL)


def kernel(x_prompt, x_sample, meta_tokens, ln_emb_g, ln_emb_b, w_in, na_rpb, sw_sink, w_proj_na, w_proj_sw, w_out, ln1_g, ln1_b, w_ffn_in, ffn_conv_w, ffn_conv_b, w_ffn_down, ln2_g, ln2_b):
    assert DEPTH == 1 and w_in.shape[0] == DEPTH
    row = lambda v: v.reshape(1, -1)
    lg, lb = row(ln_emb_g), row(ln_emb_b)
    wts = _layer_weights(w_in[0], w_proj_na[0], w_proj_sw[0], w_out[0], w_ffn_in[0], w_ffn_down[0])
    tables = (_na_bias(na_rpb[0]), _sw_bias(sw_sink[0]), _sw_meta_bias(sw_sink[0]))
    pm = _inproj(meta_tokens, lg, lb, wts[0], tm=N_META)
    args = (meta_tokens, pm, lg, lb, wts, tables, row(ln1_g[0]), row(ln1_b[0]),
            ffn_conv_w[0], row(ffn_conv_b[0]), row(ln2_g[0]), row(ln2_b[0]))
    return (_trunk(x_prompt, *args), _trunk(x_sample, *args))
```

```python
import functools

import numpy as np
import jax
import jax.numpy as jnp
from jax import lax
from jax.experimental import pallas as pl
from jax.experimental.pallas import tpu as pltpu

D_MODEL = 2048
N_META = 16
GRID_W = 64
NA_HEADS = 8
HEAD_DIM = 128
NA_WIN_ROWS = 8
NA_WIN_COLS = 16
SW_Q_HEADS = 8
SW_KV_HEADS = 2
SW_GROUP = SW_Q_HEADS // SW_KV_HEADS
SW_WINDOW = 128
SW_BLOCK = 128
D_FF = 5632
LN_EPS = 1e-5
NEG_INF = -1e30
DEPTH = 1
ALPHA = (2 * DEPTH) ** 0.25
SCALE = HEAD_DIM ** -0.5

LANE = 128
BF16_ROWS = 16
VMEM_LIMIT = 56 * 1024 * 1024

SLAB_GA, SLAB_GB, SLAB_QA, SLAB_KA, SLAB_VA, SLAB_QB, SLAB_KB, SLAB_VB = 0, 16, 32, 40, 48, 56, 64, 66
N_SLABS = 68
ORIG_QKV_COLS = 4608

NA_QROWS = 4
NA_WROWS = 11
NA_QBLK = NA_QROWS * GRID_W
NA_WKEYS = NA_WROWS * GRID_W
NA_KEYS = 768
SW_WKEYS = 3 * SW_BLOCK
SW_KEYS = 512
SW_MKEYS = 256
ATTN_UNROLL = 4

F32 = jnp.float32
BF16 = jnp.bfloat16


def _ln_rows(x, g, b):
    mu = jnp.mean(x, axis=-1, keepdims=True)
    xc = x - mu
    var = jnp.mean(xc * xc, axis=-1, keepdims=True)
    return xc * lax.rsqrt(var + LN_EPS) * g + b


def _softmax_pv(q, kcat, vcat, bias):
    s = lax.dot_general(q, kcat, (((1,), (1,)), ((), ())), preferred_element_type=F32)
    s = s * SCALE
    if bias is not None:
        s = s + bias
    m = jnp.max(s, axis=-1, keepdims=True)
    e = jnp.exp(s - m)
    l = jnp.sum(e, axis=-1, keepdims=True)
    o = jnp.dot(e.astype(BF16), vcat, preferred_element_type=F32)
    return o / l


def _inproj_kernel(x_ref, g_ref, b_ref, w_ref, o_ref, xn_ref, *, chunks):
    @pl.when(pl.program_id(1) == 0)
    def _():
        xn_ref[...] = _ln_rows(x_ref[...], g_ref[...], b_ref[...]).astype(BF16)

    xn = xn_ref[...]
    for c0, cw in chunks:
        r = jnp.dot(xn, w_ref[:, c0:c0 + cw], preferred_element_type=F32)
        for s in range(cw // LANE):
            o_ref[c0 // LANE + s] = r[:, s * LANE:(s + 1) * LANE].astype(BF16)


def _inproj(x2, g, b, w, tm):
    m = x2.shape[0]
    n_tiles = 4
    tn = w.shape[1] // n_tiles
    slabs = tn // LANE
    chunks = tuple((c, min(512, tn - c)) for c in range(0, tn, 512))
    return pl.pallas_call(
        functools.partial(_inproj_kernel, chunks=chunks),
        out_shape=jax.ShapeDtypeStruct((N_SLABS, m, LANE), BF16),
        grid=(m // tm, n_tiles),
        in_specs=[
            pl.BlockSpec((tm, D_MODEL), lambda i, j: (i, 0)),
            pl.BlockSpec((1, D_MODEL), lambda i, j: (0, 0)),
            pl.BlockSpec((1, D_MODEL), lambda i, j: (0, 0)),
            pl.BlockSpec((D_MODEL, tn), lambda i, j: (0, j)),
        ],
        out_specs=pl.BlockSpec((slabs, tm, LANE), lambda i, j: (j, i, 0)),
        scratch_shapes=[pltpu.VMEM((tm, D_MODEL), BF16)],
        compiler_params=pltpu.CompilerParams(
            dimension_semantics=("arbitrary", "arbitrary"), vmem_limit_bytes=VMEM_LIMIT),
        name="inproj",
    )(x2, g, b, w)


def _na_kernel(q_ref, k_ref, v_ref, km_ref, vm_ref, bias_ref, o_ref, kcat_ref, vcat_ref, *, rows, nblk):
    pad = jnp.zeros((NA_KEYS - NA_WKEYS - N_META, HEAD_DIM), BF16)
    for u in range(ATTN_UNROLL):
        kcat_ref[u, NA_WKEYS:NA_WKEYS + N_META, :] = km_ref[0]
        vcat_ref[u, NA_WKEYS:NA_WKEYS + N_META, :] = vm_ref[0]
        kcat_ref[u, NA_WKEYS + N_META:, :] = pad
        vcat_ref[u, NA_WKEYS + N_META:, :] = pad

    def body(it, carry):
        for u in range(ATTN_UNROLL):
            i = it * ATTN_UNROLL + u
            r0 = i * NA_QROWS
            w0 = jnp.minimum(jnp.clip(r0 - NA_WIN_ROWS // 2, 0, rows - NA_WIN_ROWS), rows - NA_WROWS)
            start = pl.multiple_of(w0 * GRID_W, GRID_W)
            kcat_ref[u, 0:NA_WKEYS, :] = k_ref[0, pl.ds(start, NA_WKEYS), :]
            vcat_ref[u, 0:NA_WKEYS, :] = v_ref[0, pl.ds(start, NA_WKEYS), :]
            variant = jnp.where(i == 0, 0, jnp.where(i == nblk - 1, 2, 1))
            qs = pl.multiple_of(i * NA_QBLK, NA_QBLK)
            q = q_ref[0, pl.ds(qs, NA_QBLK), :]
            o = _softmax_pv(q, kcat_ref[u], vcat_ref[u], bias_ref[variant, 0])
            o_ref[0, pl.ds(qs, NA_QBLK), :] = o.astype(BF16)
        return carry

    lax.fori_loop(0, nblk // ATTN_UNROLL, body, 0)


def _na_attn(p, pm, bias, b_sz, t):
    rows = t // GRID_W
    nblk = rows // NA_QROWS
    assert rows % NA_QROWS == 0 and nblk >= 3 and rows >= NA_WROWS and nblk % ATTN_UNROLL == 0
    seq = lambda slab: pl.BlockSpec((1, t, HEAD_DIM), lambda b, h: (slab + h, b, 0))
    meta = lambda slab: pl.BlockSpec((1, N_META, HEAD_DIM), lambda b, h: (slab + h, 0, 0))
    return pl.pallas_call(
        functools.partial(_na_kernel, rows=rows, nblk=nblk),
        out_shape=jax.ShapeDtypeStruct((NA_HEADS, b_sz * t, HEAD_DIM), BF16),
        grid=(b_sz, NA_HEADS),
        in_specs=[seq(SLAB_QA), seq(SLAB_KA), seq(SLAB_VA), meta(SLAB_KA), meta(SLAB_VA),
                  pl.BlockSpec((3, 1, NA_QBLK, NA_KEYS), lambda b, h: (0, h, 0, 0))],
        out_specs=pl.BlockSpec((1, t, HEAD_DIM), lambda b, h: (h, b, 0)),
        scratch_shapes=[pltpu.VMEM((ATTN_UNROLL, NA_KEYS, HEAD_DIM), BF16)] * 2,
        compiler_params=pltpu.CompilerParams(
            dimension_semantics=("arbitrary", "arbitrary"), vmem_limit_bytes=VMEM_LIMIT),
        name="na_attn",
    )(p, p, p, pm, pm, bias)


def _na_bias(rpb, dtype=F32):
    edge = GRID_W - NA_WIN_COLS
    ext = jnp.pad(rpb.astype(dtype), ((0, 0), (0, 0), (edge, edge)), mode="edge")
    cols = jnp.stack([ext[:, :, GRID_W - 1 - qc:2 * GRID_W - 1 - qc] for qc in range(GRID_W)], axis=2)
    rpad = NA_WROWS - NA_WIN_ROWS
    cols = jnp.pad(cols, ((0, 0), (rpad, rpad), (0, 0), (0, 0)))
    qc = np.arange(GRID_W)[:, None, None]
    j = np.arange(NA_WROWS)[None, :, None]
    kc = np.arange(GRID_W)[None, None, :]
    col_start = np.clip(qc - NA_WIN_COLS // 2, 0, GRID_W - NA_WIN_COLS)
    in_cols = (kc >= col_start) & (kc < col_start + NA_WIN_COLS)
    tables = []
    variants = ([(i, 0) for i in range(NA_QROWS)],
                [(i + NA_WIN_ROWS // 2, i) for i in range(NA_QROWS)],
                [(i + NA_WROWS - NA_QROWS, NA_WROWS - NA_WIN_ROWS) for i in range(NA_QROWS)])
    for variant in variants:
        per_row = []
        for qoff, rs in variant:
            r_lo = (NA_WIN_ROWS - 1) - qoff + rpad
            blk = jnp.transpose(cols[:, r_lo:r_lo + NA_WROWS], (0, 2, 1, 3))
            mask = ((j >= rs) & (j < rs + NA_WIN_ROWS)) & in_cols
            per_row.append(jnp.where(mask[None], blk, NEG_INF))
        tables.append(jnp.stack(per_row, axis=1).reshape(NA_HEADS, NA_QBLK, NA_WKEYS))
    tab = jnp.stack(tables)
    lead = tab.shape[:-1]
    return jnp.concatenate([tab, jnp.zeros(lead + (N_META,), dtype),
                            jnp.full(lead + (NA_KEYS - NA_WKEYS - N_META,), NEG_INF, dtype)], axis=-1)


def _swa_kernel(q_ref, k_ref, v_ref, km_ref, vm_ref, bias_ref, o_ref, kcat_ref, vcat_ref, *, t, nb, bps):
    pad = jnp.zeros((SW_KEYS - SW_WKEYS - N_META, HEAD_DIM), BF16)
    for u in range(ATTN_UNROLL):
        kcat_ref[u, SW_WKEYS:SW_WKEYS + N_META, :] = km_ref[0]
        vcat_ref[u, SW_WKEYS:SW_WKEYS + N_META, :] = vm_ref[0]
        kcat_ref[u, SW_WKEYS + N_META:, :] = pad
        vcat_ref[u, SW_WKEYS + N_META:, :] = pad
    step = pl.program_id(2)

    def body(it, carry):
        for u in range(ATTN_UNROLL):
            i = it * ATTN_UNROLL + u
            n = step * bps + i
            start = pl.multiple_of(jnp.clip((n - 1) * SW_BLOCK, 0, t - SW_WKEYS), SW_BLOCK)
            kcat_ref[u, 0:SW_WKEYS, :] = k_ref[0, pl.ds(start, SW_WKEYS), :]
            vcat_ref[u, 0:SW_WKEYS, :] = v_ref[0, pl.ds(start, SW_WKEYS), :]
            variant = jnp.where(n == 0, 0, jnp.where(n == nb - 1, 2, 1))
            qs = pl.multiple_of(i * SW_BLOCK, SW_BLOCK)
            q = q_ref[:, pl.ds(qs, SW_BLOCK), :].reshape(SW_GROUP * SW_BLOCK, HEAD_DIM)
            bias = bias_ref[variant].reshape(SW_GROUP * SW_BLOCK, SW_KEYS)
            o = _softmax_pv(q, kcat_ref[u], vcat_ref[u], bias)
            o_ref[:, pl.ds(qs, SW_BLOCK), :] = o.reshape(SW_GROUP, SW_BLOCK, HEAD_DIM).astype(BF16)
        return carry

    lax.fori_loop(0, bps // ATTN_UNROLL, body, 0)


def _swa_attn(p, pm, bias, b_sz, t, tq):
    nb = t // SW_BLOCK
    assert nb >= 3 and t % tq == 0 and (tq // SW_BLOCK) % ATTN_UNROLL == 0
    steps = t // tq
    bps = tq // SW_BLOCK
    kv = lambda slab: pl.BlockSpec((1, t, HEAD_DIM), lambda b, g, s: (slab + g, b, 0))
    meta = lambda slab: pl.BlockSpec((1, N_META, HEAD_DIM), lambda b, g, s: (slab + g, 0, 0))
    return pl.pallas_call(
        functools.partial(_swa_kernel, t=t, nb=nb, bps=bps),
        out_shape=jax.ShapeDtypeStruct((SW_Q_HEADS, b_sz * t, HEAD_DIM), BF16),
        grid=(b_sz, SW_KV_HEADS, steps),
        in_specs=[pl.BlockSpec((SW_GROUP, tq, HEAD_DIM), lambda b, g, s: (SLAB_QB // SW_GROUP + g, b * steps + s, 0)),
                  kv(SLAB_KB), kv(SLAB_VB), meta(SLAB_KB), meta(SLAB_VB),
                  pl.BlockSpec((3, SW_GROUP, SW_BLOCK, SW_KEYS), lambda b, g, s: (0, g, 0, 0))],
        out_specs=pl.BlockSpec((SW_GROUP, tq, HEAD_DIM), lambda b, g, s: (g, b * steps + s, 0)),
        scratch_shapes=[pltpu.VMEM((ATTN_UNROLL, SW_KEYS, HEAD_DIM), BF16)] * 2,
        compiler_params=pltpu.CompilerParams(
            dimension_semantics=("arbitrary", "arbitrary", "arbitrary"), vmem_limit_bytes=VMEM_LIMIT),
        name="swa_attn",
    )(p, p, p, pm, pm, bias)


def _sw_slopes():
    return np.power(2.0, -8.0 * np.arange(1, SW_Q_HEADS + 1, dtype=np.float64) / SW_Q_HEADS).astype(np.float32)


def _sw_bias(sink, dtype=F32):
    i = np.arange(SW_BLOCK)[:, None]
    j = np.arange(SW_WKEYS)[None, :]
    slopes = jnp.asarray(_sw_slopes(), dtype)[:, None, None]
    tables = []
    for qo in (0, SW_BLOCK, 2 * SW_BLOCK):
        dist = np.abs(qo + i - j)
        pen = -(jnp.asarray(dist, dtype)[None] * slopes)
        tables.append(jnp.where((dist <= SW_WINDOW)[None], pen, NEG_INF))
    tab = jnp.stack(tables)
    lead = tab.shape[:-1]
    snk = jnp.broadcast_to(sink.astype(dtype)[None, :, None, None], lead + (1,))
    return jnp.concatenate([tab, jnp.zeros(lead + (N_META,), dtype), snk,
                            jnp.full(lead + (SW_KEYS - SW_WKEYS - N_META - 1,), NEG_INF, dtype)], axis=-1)


def _sw_meta_bias(sink, dtype=F32):
    q = np.arange(N_META)[:, None]
    k = np.arange(SW_BLOCK)[None, :]
    dist = (N_META + k) - q
    slopes = jnp.asarray(_sw_slopes(), dtype)[:, None, None]
    pen = jnp.where((dist <= SW_WINDOW)[None], -(slopes * jnp.asarray(dist, dtype)[None]), NEG_INF)
    lead = pen.shape[:-1]
    snk = jnp.broadcast_to(sink.astype(dtype)[:, None, None], lead + (1,))
    return jnp.concatenate([jnp.zeros(lead + (N_META,), dtype), pen, snk,
                            jnp.full(lead + (SW_MKEYS - N_META - SW_BLOCK - 1,), NEG_INF, dtype)], axis=-1)


def _meta_attn_kernel(pm_ref, k0_ref, v0_ref, bias_ref, oa_ref, ob_ref, kcat_ref, vcat_ref):
    for h in range(NA_HEADS):
        o = _softmax_pv(pm_ref[SLAB_QA + h], pm_ref[SLAB_KA + h], pm_ref[SLAB_VA + h], None)
        oa_ref[h] = o.astype(BF16)
    pad = jnp.zeros((SW_MKEYS - N_META - SW_BLOCK, HEAD_DIM), BF16)
    for g in range(SW_KV_HEADS):
        kcat_ref[0:N_META, :] = pm_ref[SLAB_KB + g]
        vcat_ref[0:N_META, :] = pm_ref[SLAB_VB + g]
        kcat_ref[N_META:N_META + SW_BLOCK, :] = k0_ref[g]
        vcat_ref[N_META:N_META + SW_BLOCK, :] = v0_ref[g]
        kcat_ref[N_META + SW_BLOCK:, :] = pad
        vcat_ref[N_META + SW_BLOCK:, :] = pad
        q = jnp.concatenate([pm_ref[SLAB_QB + g * SW_GROUP + r] for r in range(SW_GROUP)], axis=0)
        bias = bias_ref[g * SW_GROUP:(g + 1) * SW_GROUP].reshape(SW_GROUP * N_META, SW_MKEYS)
        o = _softmax_pv(q, kcat_ref[...], vcat_ref[...], bias)
        for r in range(SW_GROUP):
            ob_ref[g * SW_GROUP + r] = o[r * N_META:(r + 1) * N_META].astype(BF16)


def _meta_attn(p, pm, bias, b_sz, t):
    blocks_per_seq = t // SW_BLOCK
    first_block = lambda slab: pl.BlockSpec((SW_KV_HEADS, SW_BLOCK, HEAD_DIM),
                                            lambda b: (slab // SW_KV_HEADS, b * blocks_per_seq, 0))
    out = jax.ShapeDtypeStruct((NA_HEADS, b_sz * N_META, HEAD_DIM), BF16)
    return pl.pallas_call(
        _meta_attn_kernel,
        out_shape=(out, out),
        grid=(b_sz,),
        in_specs=[pl.BlockSpec((N_SLABS, N_META, HEAD_DIM), lambda b: (0, 0, 0)),
                  first_block(SLAB_KB), first_block(SLAB_VB),
                  pl.BlockSpec((SW_Q_HEADS, N_META, SW_MKEYS), lambda b: (0, 0, 0))],
        out_specs=(pl.BlockSpec((NA_HEADS, N_META, HEAD_DIM), lambda b: (0, b, 0)),
                   pl.BlockSpec((SW_Q_HEADS, N_META, HEAD_DIM), lambda b: (0, b, 0))),
        scratch_shapes=[pltpu.VMEM((SW_MKEYS, HEAD_DIM), BF16), pltpu.VMEM((SW_MKEYS, HEAD_DIM), BF16)],
        compiler_params=pltpu.CompilerParams(dimension_semantics=("arbitrary",)),
        name="meta_attn",
    )(pm, p, p, bias)


def _post_kernel(x_ref, lg_ref, lb_ref, oa_ref, ob_ref, ga_ref, gb_ref, wna_ref, wsw_ref, wout_ref,
                 g1_ref, b1_ref, h_ref):
    h0 = _ln_rows(x_ref[...], lg_ref[...], lb_ref[...])
    oa = jnp.concatenate([oa_ref[h] for h in range(NA_HEADS)], axis=1)
    ob = jnp.concatenate([ob_ref[h] for h in range(SW_Q_HEADS)], axis=1)
    a = jnp.dot(oa, wna_ref[...], preferred_element_type=F32)
    b = jnp.dot(ob, wsw_ref[...], preferred_element_type=F32)
    n_g = D_MODEL // LANE
    ga = jnp.concatenate([ga_ref[c] for c in range(n_g)], axis=1).astype(F32)
    gb = jnp.concatenate([gb_ref[c] for c in range(n_g)], axis=1).astype(F32)
    merged = jax.nn.sigmoid(ga) * a + jax.nn.sigmoid(gb) * b
    y = jnp.dot(merged.astype(BF16), wout_ref[...], preferred_element_type=F32)
    h_ref[...] = _ln_rows(ALPHA * h0 + y, g1_ref[...], b1_ref[...])


def _post_attn(x2, lg, lb, oa, ob, gates, wna, wsw, wout, g1, b1, tm):
    m = x2.shape[0]
    n_g = D_MODEL // LANE
    const = lambda shape: pl.BlockSpec(shape, lambda i: (0,) * len(shape), pipeline_mode=pl.Buffered(1))
    heads = pl.BlockSpec((NA_HEADS, tm, HEAD_DIM), lambda i: (0, i, 0))
    return pl.pallas_call(
        _post_kernel,
        out_shape=jax.ShapeDtypeStruct((m, D_MODEL), F32),
        grid=(m // tm,),
        in_specs=[pl.BlockSpec((tm, D_MODEL), lambda i: (i, 0)),
                  const((1, D_MODEL)), const((1, D_MODEL)),
                  heads, heads,
                  pl.BlockSpec((n_g, tm, LANE), lambda i: (SLAB_GA // n_g, i, 0)),
                  pl.BlockSpec((n_g, tm, LANE), lambda i: (SLAB_GB // n_g, i, 0)),
                  const(wna.shape), const(wsw.shape), const(wout.shape),
                  const((1, D_MODEL)), const((1, D_MODEL))],
        out_specs=pl.BlockSpec((tm, D_MODEL), lambda i: (i, 0)),
        compiler_params=pltpu.CompilerParams(dimension_semantics=("arbitrary",), vmem_limit_bytes=VMEM_LIMIT),
        name="post_attn",
    )(x2, lg, lb, oa, ob, gates, gates, wna, wsw, wout, g1, b1)


HALO = BF16_ROWS
FFN_SUB = 256


def _ffn_kernel(h_ref, hp_ref, hn_ref, hm_ref, wg_ref, wv_ref, cw_ref, cb_ref, wd_ref, g2_ref, b2_ref,
                o_ref, hb_ref, *, tm, tiles_per_seq):
    i = pl.program_id(0)
    f = pl.program_id(1)
    nf = pl.num_programs(1)

    @pl.when(f == 0)
    def _():
        pos = i % tiles_per_seq
        prev = jnp.where(pos == 0, hm_ref[...], hp_ref[...])
        nxt = jnp.where(pos == tiles_per_seq - 1, 0.0, hn_ref[...])
        hb_ref[0:HALO, :] = prev.astype(BF16)
        hb_ref[HALO:HALO + tm, :] = h_ref[...].astype(BF16)
        hb_ref[HALO + tm:, :] = nxt.astype(BF16)
        o_ref[...] = jnp.zeros_like(o_ref)

    n = tm + 2 * HALO
    tf = wg_ref.shape[1]
    part = None
    for c0 in range(0, tf, FFN_SUB):
        sl = slice(c0, c0 + FFN_SUB)
        gp = jnp.dot(hb_ref[...], wg_ref[:, sl], preferred_element_type=F32)
        vl = jnp.dot(hb_ref[HALO:HALO + tm, :], wv_ref[:, sl], preferred_element_type=F32)
        up = pltpu.roll(gp, 1, 0)
        dn = pltpu.roll(gp, n - 1, 0)
        gate = (up * cw_ref[0:1, sl] + gp * cw_ref[1:2, sl] + dn * cw_ref[2:3, sl])[HALO:HALO + tm] + cb_ref[:, sl]
        act = (jax.nn.gelu(gate) * vl).astype(BF16)
        d = jnp.dot(act, wd_ref[sl, :], preferred_element_type=F32)
        part = d if part is None else part + d
    o_ref[...] += part

    @pl.when(f == nf - 1)
    def _():
        o_ref[...] = _ln_rows(ALPHA * h_ref[...] + o_ref[...], g2_ref[...], b2_ref[...])


def _ffn(h, h_meta, w_in, cw, cb, wd, g2, b2, t, tm, tf):
    m = h.shape[0]
    tiles_per_seq = t // tm
    nf = D_FF // tf
    hb = tm // HALO
    last_hb = m // HALO - 1
    return pl.pallas_call(
        functools.partial(_ffn_kernel, tm=tm, tiles_per_seq=tiles_per_seq),
        out_shape=jax.ShapeDtypeStruct((m, D_MODEL), F32),
        grid=(m // tm, nf),
        in_specs=[pl.BlockSpec((tm, D_MODEL), lambda i, f: (i, 0)),
                  pl.BlockSpec((HALO, D_MODEL), lambda i, f: (jnp.maximum(i * hb - 1, 0), 0)),
                  pl.BlockSpec((HALO, D_MODEL), lambda i, f: (jnp.minimum((i + 1) * hb, last_hb), 0)),
                  pl.BlockSpec((N_META, D_MODEL), lambda i, f: (i // tiles_per_seq, 0)),
                  pl.BlockSpec((D_MODEL, tf), lambda i, f: (0, f)),
                  pl.BlockSpec((D_MODEL, tf), lambda i, f: (0, f + nf)),
                  pl.BlockSpec((3, tf), lambda i, f: (0, f)),
                  pl.BlockSpec((1, tf), lambda i, f: (0, f)),
                  pl.BlockSpec((tf, D_MODEL), lambda i, f: (f, 0)),
                  pl.BlockSpec((1, D_MODEL), lambda i, f: (0, 0)),
                  pl.BlockSpec((1, D_MODEL), lambda i, f: (0, 0))],
        out_specs=pl.BlockSpec((tm, D_MODEL), lambda i, f: (i, 0)),
        scratch_shapes=[pltpu.VMEM((tm + 2 * HALO, D_MODEL), BF16)],
        compiler_params=pltpu.CompilerParams(
            dimension_semantics=("arbitrary", "arbitrary"), vmem_limit_bytes=VMEM_LIMIT),
        name="ffn",
    )(h, h, h, h_meta, w_in, w_in, cw, cb, wd, g2, b2)


def _layer_weights(w_in, w_proj_na, w_proj_sw, w_out, w_ffn_in, w_ffn_down):
    w_in_p = jnp.concatenate([w_in[:, ORIG_QKV_COLS:], w_in[:, :ORIG_QKV_COLS]], axis=1).astype(BF16)
    return (w_in_p, w_proj_na.astype(BF16), w_proj_sw.astype(BF16), w_out.astype(BF16),
            w_ffn_in.astype(BF16), w_ffn_down.astype(BF16))


def _trunk(x, meta_tokens, pm, lg, lb, wts, tables, g1, b1, cw, cb, g2, b2):
    b_sz, t, _ = x.shape
    w_in_p, wna, wsw, wout, wffn, wdown = wts
    na_bias, sw_bias, swm_bias = tables
    x2 = x.reshape(b_sz * t, D_MODEL)
    p = _inproj(x2, lg, lb, w_in_p, tm=512)
    oa = _na_attn(p, pm, na_bias, b_sz, t)
    ob = _swa_attn(p, pm, sw_bias, b_sz, t, tq=2048)
    oa_m, ob_m = _meta_attn(p, pm, swm_bias, b_sz, t)
    h = _post_attn(x2, lg, lb, oa, ob, p, wna, wsw, wout, g1, b1, tm=256)
    x_m = jnp.tile(meta_tokens, (b_sz, 1))
    gates_m = jnp.tile(pm[:SLAB_QA], (1, b_sz, 1))
    h_m = _post_attn(x_m, lg, lb, oa_m, ob_m, gates_m, wna, wsw, wout, g1, b1, tm=b_sz * N_META)
    y = _ffn(h, h_m, wffn, cw, cb, wdown, g2, b2, t, tm=512, tf=512)
    return y.reshape(b_sz, t, D_MODEL)


def kernel(x_prompt, x_sample, meta_tokens, ln_emb_g, ln_emb_b, w_in, na_rpb, sw_sink, w_proj_na, w_proj_sw, w_out, ln1_g, ln1_b, w_ffn_in, ffn_conv_w, ffn_conv_b, w_ffn_down, ln2_g, ln2_b):
    assert DEPTH == 1 and w_in.shape[0] == DEPTH
    row = lambda v: v.reshape(1, -1)
    lg, lb = row(ln_emb_g), row(ln_emb_b)
    wts = _layer_weights(w_in[0], w_proj_na[0], w_proj_sw[0], w_out[0], w_ffn_in[0], w_ffn_down[0])
    tables = (_na_bias(na_rpb[0]), _sw_bias(sw_sink[0]), _sw_meta_bias(sw_sink[0]))
    pm = _inproj(meta_tokens, lg, lb, wts[0], tm=N_META)
    args = (meta_tokens, pm, lg, lb, wts, tables, row(ln1_g[0]), row(ln1_b[0]),
            ffn_conv_w[0], row(ffn_conv_b[0]), row(ln2_g[0]), row(ln2_b[0]))
    return (_trunk(x_prompt, *args), _trunk(x_sample, *args))
```

```python
import functools

import numpy as np
import jax
import jax.numpy as jnp
from jax import lax
from jax.experimental import pallas as pl
from jax.experimental.pallas import tpu as pltpu

D_MODEL = 2048
N_META = 16
GRID_W = 64
NA_HEADS = 8
HEAD_DIM = 128
NA_WIN_ROWS = 8
NA_WIN_COLS = 16
SW_Q_HEADS = 8
SW_KV_HEADS = 2
SW_GROUP = SW_Q_HEADS // SW_KV_HEADS
SW_WINDOW = 128
SW_BLOCK = 128
D_FF = 5632
LN_EPS = 1e-5
NEG_INF = -1e30
DEPTH = 1
ALPHA = (2 * DEPTH) ** 0.25
SCALE = HEAD_DIM ** -0.5

LANE = 128
BF16_ROWS = 16
MXU_COLS = 256
VMEM_LIMIT = 56 * 1024 * 1024

SLAB_QA, SLAB_KA, SLAB_VA, SLAB_QB, SLAB_KB, SLAB_VB = 0, 8, 16, 24, 32, 34
QKV_COLS = 4608
QKV_SLABS = QKV_COLS // LANE
GATE_COLS = 2 * D_MODEL
GATE_SLABS = GATE_COLS // LANE
PROJ_CHUNK = 2 * MXU_COLS

NA_QROWS = 4
NA_WROWS = 11
NA_QBLK = NA_QROWS * GRID_W
NA_WKEYS = NA_WROWS * GRID_W
NA_KEYS = 768
SW_WKEYS = 3 * SW_BLOCK
SW_KEYS = 512
SW_MKEYS = 256
ATTN_UNROLL = 4

F32 = jnp.float32
BF16 = jnp.bfloat16


def _ln_rows(x, g, b):
    mu = jnp.mean(x, axis=-1, keepdims=True)
    xc = x - mu
    var = jnp.mean(xc * xc, axis=-1, keepdims=True)
    return xc * lax.rsqrt(var + LN_EPS) * g + b


def _softmax_pv(q, kcat, vcat, bias):
    s = lax.dot_general(q, kcat, (((1,), (1,)), ((), ())), preferred_element_type=F32)
    s = s * SCALE
    if bias is not None:
        s = s + bias
    m = jnp.max(s, axis=-1, keepdims=True)
    e = jnp.exp(s - m)
    l = jnp.sum(e, axis=-1, keepdims=True)
    o = jnp.dot(e.astype(BF16), vcat, preferred_element_type=F32)
    return o / l


def _project_to_slabs(xn, w_ref, o_ref, rows):
    for c0 in range(0, w_ref.shape[1], PROJ_CHUNK):
        r = jnp.dot(xn, w_ref[:, c0:c0 + PROJ_CHUNK], preferred_element_type=F32)
        for s in range(PROJ_CHUNK // LANE):
            o_ref[c0 // LANE + s, rows, :] = r[:, s * LANE:(s + 1) * LANE].astype(BF16)


def _ln_gates_kernel(x_ref, g_ref, b_ref, w_ref, xn_ref, o_ref, *, parts):
    hr = x_ref.shape[0] // parts
    for r in range(parts):
        rows = slice(r * hr, (r + 1) * hr)
        xn = _ln_rows(x_ref[rows, :], g_ref[...], b_ref[...]).astype(BF16)
        xn_ref[rows, :] = xn
        _project_to_slabs(xn, w_ref, o_ref, rows)


def _ln_gates(x2, g, b, w, tm, parts):
    m = x2.shape[0]
    const = lambda shape: pl.BlockSpec(shape, lambda i: (0,) * len(shape), pipeline_mode=pl.Buffered(1))
    return pl.pallas_call(
        functools.partial(_ln_gates_kernel, parts=parts),
        out_shape=(jax.ShapeDtypeStruct((m, D_MODEL), BF16),
                   jax.ShapeDtypeStruct((GATE_SLABS, m, LANE), BF16)),
        grid=(m // tm,),
        in_specs=[pl.BlockSpec((tm, D_MODEL), lambda i: (i, 0)),
                  const((1, D_MODEL)), const((1, D_MODEL)), const(w.shape)],
        out_specs=(pl.BlockSpec((tm, D_MODEL), lambda i: (i, 0)),
                   pl.BlockSpec((GATE_SLABS, tm, LANE), lambda i: (0, i, 0))),
        compiler_params=pltpu.CompilerParams(dimension_semantics=("arbitrary",), vmem_limit_bytes=VMEM_LIMIT),
        name="ln_gates",
    )(x2, g, b, w)


def _qkv_kernel(xn_ref, w_ref, o_ref):
    _project_to_slabs(xn_ref[...], w_ref, o_ref, slice(None))


def _qkv_proj(xn, w, tm):
    m = xn.shape[0]
    return pl.pallas_call(
        _qkv_kernel,
        out_shape=jax.ShapeDtypeStruct((QKV_SLABS, m, LANE), BF16),
        grid=(m // tm,),
        in_specs=[pl.BlockSpec((tm, D_MODEL), lambda i: (i, 0)),
                  pl.BlockSpec(w.shape, lambda i: (0, 0), pipeline_mode=pl.Buffered(1))],
        out_specs=pl.BlockSpec((QKV_SLABS, tm, LANE), lambda i: (0, i, 0)),
        compiler_params=pltpu.CompilerParams(dimension_semantics=("arbitrary",), vmem_limit_bytes=VMEM_LIMIT),
        name="qkv_proj",
    )(xn, w)


def _na_kernel(q_ref, k_ref, v_ref, km_ref, vm_ref, bias_ref, o_ref, kcat_ref, vcat_ref, *, rows, nblk):
    pad = jnp.zeros((NA_KEYS - NA_WKEYS - N_META, HEAD_DIM), BF16)
    for u in range(ATTN_UNROLL):
        kcat_ref[u, NA_WKEYS:NA_WKEYS + N_META, :] = km_ref[0]
        vcat_ref[u, NA_WKEYS:NA_WKEYS + N_META, :] = vm_ref[0]
        kcat_ref[u, NA_WKEYS + N_META:, :] = pad
        vcat_ref[u, NA_WKEYS + N_META:, :] = pad

    def body(it, carry):
        for u in range(ATTN_UNROLL):
            i = it * ATTN_UNROLL + u
            r0 = i * NA_QROWS
            w0 = jnp.minimum(jnp.clip(r0 - NA_WIN_ROWS // 2, 0, rows - NA_WIN_ROWS), rows - NA_WROWS)
            start = pl.multiple_of(w0 * GRID_W, GRID_W)
            kcat_ref[u, 0:NA_WKEYS, :] = k_ref[0, pl.ds(start, NA_WKEYS), :]
            vcat_ref[u, 0:NA_WKEYS, :] = v_ref[0, pl.ds(start, NA_WKEYS), :]
            variant = jnp.where(i == 0, 0, jnp.where(i == nblk - 1, 2, 1))
            qs = pl.multiple_of(i * NA_QBLK, NA_QBLK)
            q = q_ref[0, pl.ds(qs, NA_QBLK), :]
            o = _softmax_pv(q, kcat_ref[u], vcat_ref[u], bias_ref[variant, 0])
            o_ref[0, pl.ds(qs, NA_QBLK), :] = o.astype(BF16)
        return carry

    lax.fori_loop(0, nblk // ATTN_UNROLL, body, 0)


def _na_attn(p, pm, bias, b_sz, t):
    rows = t // GRID_W
    nblk = rows // NA_QROWS
    assert rows % NA_QROWS == 0 and nblk >= 3 and rows >= NA_WROWS and nblk % ATTN_UNROLL == 0
    seq = lambda slab: pl.BlockSpec((1, t, HEAD_DIM), lambda b, h: (slab + h, b, 0))
    meta = lambda slab: pl.BlockSpec((1, N_META, HEAD_DIM), lambda b, h: (slab + h, 0, 0))
    return pl.pallas_call(
        functools.partial(_na_kernel, rows=rows, nblk=nblk),
        out_shape=jax.ShapeDtypeStruct((NA_HEADS, b_sz * t, HEAD_DIM), BF16),
        grid=(b_sz, NA_HEADS),
        in_specs=[seq(SLAB_QA), seq(SLAB_KA), seq(SLAB_VA), meta(SLAB_KA), meta(SLAB_VA),
                  pl.BlockSpec((3, 1, NA_QBLK, NA_KEYS), lambda b, h: (0, h, 0, 0))],
        out_specs=pl.BlockSpec((1, t, HEAD_DIM), lambda b, h: (h, b, 0)),
        scratch_shapes=[pltpu.VMEM((ATTN_UNROLL, NA_KEYS, HEAD_DIM), BF16)] * 2,
        compiler_params=pltpu.CompilerParams(
            dimension_semantics=("arbitrary", "arbitrary"), vmem_limit_bytes=VMEM_LIMIT),
        name="na_attn",
    )(p, p, p, pm, pm, bias)


def _na_bias(rpb, dtype=F32):
    edge = GRID_W - NA_WIN_COLS
    ext = jnp.pad(rpb.astype(dtype), ((0, 0), (0, 0), (edge, edge)), mode="edge")
    cols = jnp.stack([ext[:, :, GRID_W - 1 - qc:2 * GRID_W - 1 - qc] for qc in range(GRID_W)], axis=2)
    rpad = NA_WROWS - NA_WIN_ROWS
    cols = jnp.pad(cols, ((0, 0), (rpad, rpad), (0, 0), (0, 0)))
    qc = np.arange(GRID_W)[:, None, None]
    j = np.arange(NA_WROWS)[None, :, None]
    kc = np.arange(GRID_W)[None, None, :]
    col_start = np.clip(qc - NA_WIN_COLS // 2, 0, GRID_W - NA_WIN_COLS)
    in_cols = (kc >= col_start) & (kc < col_start + NA_WIN_COLS)
    tables = []
    variants = ([(i, 0) for i in range(NA_QROWS)],
                [(i + NA_WIN_ROWS // 2, i) for i in range(NA_QROWS)],
                [(i + NA_WROWS - NA_QROWS, NA_WROWS - NA_WIN_ROWS) for i in range(NA_QROWS)])
    for variant in variants:
        per_row = []
        for qoff, rs in variant:
            r_lo = (NA_WIN_ROWS - 1) - qoff + rpad
            blk = jnp.transpose(cols[:, r_lo:r_lo + NA_WROWS], (0, 2, 1, 3))
            mask = ((j >= rs) & (j < rs + NA_WIN_ROWS)) & in_cols
            per_row.append(jnp.where(mask[None], blk, NEG_INF))
        tables.append(jnp.stack(per_row, axis=1).reshape(NA_HEADS, NA_QBLK, NA_WKEYS))
    tab = jnp.stack(tables)
    lead = tab.shape[:-1]
    return jnp.concatenate([tab, jnp.zeros(lead + (N_META,), dtype),
                            jnp.full(lead + (NA_KEYS - NA_WKEYS - N_META,), NEG_INF, dtype)], axis=-1)


def _swa_kernel(q_ref, k_ref, v_ref, km_ref, vm_ref, bias_ref, o_ref, kcat_ref, vcat_ref, *, t, nb, bps):
    pad = jnp.zeros((SW_KEYS - SW_WKEYS - N_META, HEAD_DIM), BF16)
    for u in range(ATTN_UNROLL):
        kcat_ref[u, SW_WKEYS:SW_WKEYS + N_META, :] = km_ref[0]
        vcat_ref[u, SW_WKEYS:SW_WKEYS + N_META, :] = vm_ref[0]
        kcat_ref[u, SW_WKEYS + N_META:, :] = pad
        vcat_ref[u, SW_WKEYS + N_META:, :] = pad
    step = pl.program_id(2)

    def body(it, carry):
        for u in range(ATTN_UNROLL):
            i = it * ATTN_UNROLL + u
            n = step * bps + i
            start = pl.multiple_of(jnp.clip((n - 1) * SW_BLOCK, 0, t - SW_WKEYS), SW_BLOCK)
            kcat_ref[u, 0:SW_WKEYS, :] = k_ref[0, pl.ds(start, SW_WKEYS), :]
            vcat_ref[u, 0:SW_WKEYS, :] = v_ref[0, pl.ds(start, SW_WKEYS), :]
            variant = jnp.where(n == 0, 0, jnp.where(n == nb - 1, 2, 1))
            qs = pl.multiple_of(i * SW_BLOCK, SW_BLOCK)
            q = q_ref[:, pl.ds(qs, SW_BLOCK), :].reshape(SW_GROUP * SW_BLOCK, HEAD_DIM)
            bias = bias_ref[variant].reshape(SW_GROUP * SW_BLOCK, SW_KEYS)
            o = _softmax_pv(q, kcat_ref[u], vcat_ref[u], bias)
            o_ref[:, pl.ds(qs, SW_BLOCK), :] = o.reshape(SW_GROUP, SW_BLOCK, HEAD_DIM).astype(BF16)
        return carry

    lax.fori_loop(0, bps // ATTN_UNROLL, body, 0)


def _swa_attn(p, pm, bias, b_sz, t, tq):
    nb = t // SW_BLOCK
    assert nb >= 3 and t % tq == 0 and (tq // SW_BLOCK) % ATTN_UNROLL == 0
    steps = t // tq
    bps = tq // SW_BLOCK
    kv = lambda slab: pl.BlockSpec((1, t, HEAD_DIM), lambda b, g, s: (slab + g, b, 0))
    meta = lambda slab: pl.BlockSpec((1, N_META, HEAD_DIM), lambda b, g, s: (slab + g, 0, 0))
    return pl.pallas_call(
        functools.partial(_swa_kernel, t=t, nb=nb, bps=bps),
        out_shape=jax.ShapeDtypeStruct((SW_Q_HEADS, b_sz * t, HEAD_DIM), BF16),
        grid=(b_sz, SW_KV_HEADS, steps),
        in_specs=[pl.BlockSpec((SW_GROUP, tq, HEAD_DIM), lambda b, g, s: (SLAB_QB // SW_GROUP + g, b * steps + s, 0)),
                  kv(SLAB_KB), kv(SLAB_VB), meta(SLAB_KB), meta(SLAB_VB),
                  pl.BlockSpec((3, SW_GROUP, SW_BLOCK, SW_KEYS), lambda b, g, s: (0, g, 0, 0))],
        out_specs=pl.BlockSpec((SW_GROUP, tq, HEAD_DIM), lambda b, g, s: (g, b * steps + s, 0)),
        scratch_shapes=[pltpu.VMEM((ATTN_UNROLL, SW_KEYS, HEAD_DIM), BF16)] * 2,
        compiler_params=pltpu.CompilerParams(
            dimension_semantics=("arbitrary", "arbitrary", "arbitrary"), vmem_limit_bytes=VMEM_LIMIT),
        name="swa_attn",
    )(p, p, p, pm, pm, bias)


def _sw_slopes():
    return np.power(2.0, -8.0 * np.arange(1, SW_Q_HEADS + 1, dtype=np.float64) / SW_Q_HEADS).astype(np.float32)


def _sw_bias(sink, dtype=F32):
    i = np.arange(SW_BLOCK)[:, None]
    j = np.arange(SW_WKEYS)[None, :]
    slopes = jnp.asarray(_sw_slopes(), dtype)[:, None, None]
    tables = []
    for qo in (0, SW_BLOCK, 2 * SW_BLOCK):
        dist = np.abs(qo + i - j)
        pen = -(jnp.asarray(dist, dtype)[None] * slopes)
        tables.append(jnp.where((dist <= SW_WINDOW)[None], pen, NEG_INF))
    tab = jnp.stack(tables)
    lead = tab.shape[:-1]
    snk = jnp.broadcast_to(sink.astype(dtype)[None, :, None, None], lead + (1,))
    return jnp.concatenate([tab, jnp.zeros(lead + (N_META,), dtype), snk,
                            jnp.full(lead + (SW_KEYS - SW_WKEYS - N_META - 1,), NEG_INF, dtype)], axis=-1)


def _sw_meta_bias(sink, dtype=F32):
    q = np.arange(N_META)[:, None]
    k = np.arange(SW_BLOCK)[None, :]
    dist = (N_META + k) - q
    slopes = jnp.asarray(_sw_slopes(), dtype)[:, None, None]
    pen = jnp.where((dist <= SW_WINDOW)[None], -(slopes * jnp.asarray(dist, dtype)[None]), NEG_INF)
    lead = pen.shape[:-1]
    snk = jnp.broadcast_to(sink.astype(dtype)[:, None, None], lead + (1,))
    return jnp.concatenate([jnp.zeros(lead + (N_META,), dtype), pen, snk,
                            jnp.full(lead + (SW_MKEYS - N_META - SW_BLOCK - 1,), NEG_INF, dtype)], axis=-1)


def _meta_attn_kernel(pm_ref, k0_ref, v0_ref, bias_ref, oa_ref, ob_ref, kcat_ref, vcat_ref):
    for h in range(NA_HEADS):
        o = _softmax_pv(pm_ref[SLAB_QA + h], pm_ref[SLAB_KA + h], pm_ref[SLAB_VA + h], None)
        oa_ref[h] = o.astype(BF16)
    pad = jnp.zeros((SW_MKEYS - N_META - SW_BLOCK, HEAD_DIM), BF16)
    for g in range(SW_KV_HEADS):
        kcat_ref[0:N_META, :] = pm_ref[SLAB_KB + g]
        vcat_ref[0:N_META, :] = pm_ref[SLAB_VB + g]
        kcat_ref[N_META:N_META + SW_BLOCK, :] = k0_ref[g]
        vcat_ref[N_META:N_META + SW_BLOCK, :] = v0_ref[g]
        kcat_ref[N_META + SW_BLOCK:, :] = pad
        vcat_ref[N_META + SW_BLOCK:, :] = pad
        q = jnp.concatenate([pm_ref[SLAB_QB + g * SW_GROUP + r] for r in range(SW_GROUP)], axis=0)
        bias = bias_ref[g * SW_GROUP:(g + 1) * SW_GROUP].reshape(SW_GROUP * N_META, SW_MKEYS)
        o = _softmax_pv(q, kcat_ref[...], vcat_ref[...], bias)
        for r in range(SW_GROUP):
            ob_ref[g * SW_GROUP + r] = o[r * N_META:(r + 1) * N_META].astype(BF16)


def _meta_attn(p, pm, bias, b_sz, t):
    blocks_per_seq = t // SW_BLOCK
    first_block = lambda slab: pl.BlockSpec((SW_KV_HEADS, SW_BLOCK, HEAD_DIM),
                                            lambda b: (slab // SW_KV_HEADS, b * blocks_per_seq, 0))
    out = jax.ShapeDtypeStruct((NA_HEADS, b_sz * N_META, HEAD_DIM), BF16)
    return pl.pallas_call(
        _meta_attn_kernel,
        out_shape=(out, out),
        grid=(b_sz,),
        in_specs=[pl.BlockSpec((QKV_SLABS, N_META, HEAD_DIM), lambda b: (0, 0, 0)),
                  first_block(SLAB_KB), first_block(SLAB_VB),
                  pl.BlockSpec((SW_Q_HEADS, N_META, SW_MKEYS), lambda b: (0, 0, 0))],
        out_specs=(pl.BlockSpec((NA_HEADS, N_META, HEAD_DIM), lambda b: (0, b, 0)),
                   pl.BlockSpec((SW_Q_HEADS, N_META, HEAD_DIM), lambda b: (0, b, 0))),
        scratch_shapes=[pltpu.VMEM((SW_MKEYS, HEAD_DIM), BF16), pltpu.VMEM((SW_MKEYS, HEAD_DIM), BF16)],
        compiler_params=pltpu.CompilerParams(dimension_semantics=("arbitrary",)),
        name="meta_attn",
    )(pm, p, p, bias)


def _post_kernel(x_ref, lg_ref, lb_ref, oa_ref, ob_ref, ga_ref, gb_ref, wna_ref, wsw_ref, wout_ref,
                 g1_ref, b1_ref, h_ref, *, parts):
    hr = x_ref.shape[0] // parts
    for r in range(parts):
        rows = slice(r * hr, (r + 1) * hr)
        oa = jnp.concatenate([oa_ref[h, rows, :] for h in range(NA_HEADS)], axis=1)
        ob = jnp.concatenate([ob_ref[h, rows, :] for h in range(SW_Q_HEADS)], axis=1)
        merged = []
        for c0 in range(0, D_MODEL, PROJ_CHUNK):
            cols = slice(c0, c0 + PROJ_CHUNK)
            slabs = range(c0 // LANE, (c0 + PROJ_CHUNK) // LANE)
            a = jnp.dot(oa, wna_ref[:, cols], preferred_element_type=F32)
            b = jnp.dot(ob, wsw_ref[:, cols], preferred_element_type=F32)
            ga = jnp.concatenate([ga_ref[s, rows, :] for s in slabs], axis=1).astype(F32)
            gb = jnp.concatenate([gb_ref[s, rows, :] for s in slabs], axis=1).astype(F32)
            merged.append((jax.nn.sigmoid(ga) * a + jax.nn.sigmoid(gb) * b).astype(BF16))
        merged = jnp.concatenate(merged, axis=1)
        y = jnp.dot(merged, wout_ref[...], preferred_element_type=F32)
        h0 = _ln_rows(x_ref[rows, :], lg_ref[...], lb_ref[...])
        h_ref[rows, :] = _ln_rows(ALPHA * h0 + y, g1_ref[...], b1_ref[...])


def _post_attn(x2, lg, lb, oa, ob, gates, wna, wsw, wout, g1, b1, tm, parts):
    m = x2.shape[0]
    n_g = D_MODEL // LANE
    const = lambda shape: pl.BlockSpec(shape, lambda i: (0,) * len(shape), pipeline_mode=pl.Buffered(1))
    heads = pl.BlockSpec((NA_HEADS, tm, HEAD_DIM), lambda i: (0, i, 0))
    return pl.pallas_call(
        functools.partial(_post_kernel, parts=parts),
        out_shape=jax.ShapeDtypeStruct((m, D_MODEL), F32),
        grid=(m // tm,),
        in_specs=[pl.BlockSpec((tm, D_MODEL), lambda i: (i, 0)),
                  const((1, D_MODEL)), const((1, D_MODEL)),
                  heads, heads,
                  pl.BlockSpec((n_g, tm, LANE), lambda i: (0, i, 0)),
                  pl.BlockSpec((n_g, tm, LANE), lambda i: (1, i, 0)),
                  const(wna.shape), const(wsw.shape), const(wout.shape),
                  const((1, D_MODEL)), const((1, D_MODEL))],
        out_specs=pl.BlockSpec((tm, D_MODEL), lambda i: (i, 0)),
        compiler_params=pltpu.CompilerParams(dimension_semantics=("arbitrary",), vmem_limit_bytes=VMEM_LIMIT),
        name="post_attn",
    )(x2, lg, lb, oa, ob, gates, gates, wna, wsw, wout, g1, b1)


HALO = BF16_ROWS
FFN_SUB = MXU_COLS


def _ffn_kernel(h_ref, hn_ref, hm_ref, wgv_ref, cwb_ref, wd_ref, ln_ref, o_ref, hb_ref, *, tm, tf, tiles_per_seq):
    i = pl.program_id(0)
    f = pl.program_id(1)
    nf = pl.num_programs(1)
    pos = i % tiles_per_seq

    @pl.when(jnp.logical_and(f == 0, pos == 0))
    def _():
        hb_ref[0:HALO, :] = hm_ref[...].astype(BF16)

    @pl.when(jnp.logical_and(f == 0, pos != 0))
    def _():
        hb_ref[0:HALO, :] = hb_ref[tm:tm + HALO, :]

    @pl.when(f == 0)
    def _():
        nxt = jnp.where(pos == tiles_per_seq - 1, 0.0, hn_ref[...])
        hb_ref[HALO:HALO + tm, :] = h_ref[...].astype(BF16)
        hb_ref[HALO + tm:, :] = nxt.astype(BF16)
        o_ref[...] = jnp.zeros_like(o_ref)

    n = tm + 2 * HALO
    part = None
    for c0 in range(0, tf, FFN_SUB):
        sl = slice(c0, c0 + FFN_SUB)
        gp = jnp.dot(hb_ref[...], wgv_ref[0, :, sl], preferred_element_type=F32)
        vl = jnp.dot(hb_ref[HALO:HALO + tm, :], wgv_ref[0, :, tf + c0:tf + c0 + FFN_SUB],
                     preferred_element_type=F32)
        up = pltpu.roll(gp, 1, 0)
        dn = pltpu.roll(gp, n - 1, 0)
        gate = (up * cwb_ref[0:1, sl] + gp * cwb_ref[1:2, sl] + dn * cwb_ref[2:3, sl])[HALO:HALO + tm] + cwb_ref[3:4, sl]
        act = (jax.nn.gelu(gate) * vl).astype(BF16)
        d = jnp.dot(act, wd_ref[sl, :], preferred_element_type=F32)
        part = d if part is None else part + d
    o_ref[...] += part

    @pl.when(f == nf - 1)
    def _():
        o_ref[...] = _ln_rows(ALPHA * h_ref[...] + o_ref[...], ln_ref[0:1, :], ln_ref[1:2, :])


def _ffn(h, h_meta, wgv, cwb, wd, ln2, t, tm):
    m = h.shape[0]
    nf, _, tf2 = wgv.shape
    tf = tf2 // 2
    tiles_per_seq = t // tm
    hb = tm // HALO
    last_hb = m // HALO - 1
    return pl.pallas_call(
        functools.partial(_ffn_kernel, tm=tm, tf=tf, tiles_per_seq=tiles_per_seq),
        out_shape=jax.ShapeDtypeStruct((m, D_MODEL), F32),
        grid=(m // tm, nf),
        in_specs=[pl.BlockSpec((tm, D_MODEL), lambda i, f: (i, 0)),
                  pl.BlockSpec((HALO, D_MODEL), lambda i, f: (jnp.minimum((i + 1) * hb, last_hb), 0)),
                  pl.BlockSpec((N_META, D_MODEL), lambda i, f: (i // tiles_per_seq, 0)),
                  pl.BlockSpec((1, D_MODEL, tf2), lambda i, f: (f, 0, 0)),
                  pl.BlockSpec((4, tf), lambda i, f: (0, f)),
                  pl.BlockSpec((tf, D_MODEL), lambda i, f: (f, 0)),
                  pl.BlockSpec((2, D_MODEL), lambda i, f: (0, 0))],
        out_specs=pl.BlockSpec((tm, D_MODEL), lambda i, f: (i, 0)),
        scratch_shapes=[pltpu.VMEM((tm + 2 * HALO, D_MODEL), BF16)],
        compiler_params=pltpu.CompilerParams(
            dimension_semantics=("arbitrary", "arbitrary"), vmem_limit_bytes=VMEM_LIMIT),
        name="ffn",
    )(h, h, h_meta, wgv, cwb, wd, ln2)


def _layer_weights(w_in, w_proj_na, w_proj_sw, w_out, w_ffn_in, w_ffn_down, tf):
    nf = D_FF // tf
    wgv = w_ffn_in.astype(BF16).reshape(D_MODEL, 2, nf, tf).transpose(2, 0, 1, 3).reshape(nf, D_MODEL, 2 * tf)
    return (w_in[:, QKV_COLS:].astype(BF16), w_in[:, :QKV_COLS].astype(BF16),
            w_proj_na.astype(BF16), w_proj_sw.astype(BF16), w_out.astype(BF16), wgv, w_ffn_down.astype(BF16))


def _trunk(x, meta_tokens, pm, gm, lg, lb, wts, tables, g1, b1, cwb, ln2):
    b_sz, t, _ = x.shape
    wgate, wqkv, wna, wsw, wout, wgv, wdown = wts
    na_bias, sw_bias, swm_bias = tables
    x2 = x.reshape(b_sz * t, D_MODEL)
    xn, gates = _ln_gates(x2, lg, lb, wgate, tm=512, parts=2)
    p = _qkv_proj(xn, wqkv, tm=1024)
    oa = _na_attn(p, pm, na_bias, b_sz, t)
    ob = _swa_attn(p, pm, sw_bias, b_sz, t, tq=2048)
    oa_m, ob_m = _meta_attn(p, pm, swm_bias, b_sz, t)
    h = _post_attn(x2, lg, lb, oa, ob, gates, wna, wsw, wout, g1, b1, tm=512, parts=2)
    x_m = jnp.tile(meta_tokens, (b_sz, 1))
    gates_m = jnp.tile(gm, (1, b_sz, 1))
    h_m = _post_attn(x_m, lg, lb, oa_m, ob_m, gates_m, wna, wsw, wout, g1, b1, tm=b_sz * N_META, parts=1)
    y = _ffn(h, h_m, wgv, cwb, wdown, ln2, t, tm=512)
    return y.reshape(b_sz, t, D_MODEL)


def kernel(x_prompt, x_sample, meta_tokens, ln_emb_g, ln_emb_b, w_in, na_rpb, sw_sink, w_proj_na, w_proj_sw, w_out, ln1_g, ln1_b, w_ffn_in, ffn_conv_w, ffn_conv_b, w_ffn_down, ln2_g, ln2_b):
    assert DEPTH == 1 and w_in.shape[0] == DEPTH
    row = lambda v: v.reshape(1, -1)
    lg, lb = row(ln_emb_g), row(ln_emb_b)
    wts = _layer_weights(w_in[0], w_proj_na[0], w_proj_sw[0], w_out[0], w_ffn_in[0], w_ffn_down[0], tf=512)
    tables = (_na_bias(na_rpb[0]), _sw_bias(sw_sink[0]), _sw_meta_bias(sw_sink[0]))
    xn_m, gm = _ln_gates(meta_tokens, lg, lb, wts[0], tm=N_META, parts=1)
    pm = _qkv_proj(xn_m, wts[1], tm=N_META)
    cwb = jnp.concatenate([ffn_conv_w[0], row(ffn_conv_b[0])], axis=0)
    ln2 = jnp.stack([ln2_g[0], ln2_b[0]])
    args = (meta_tokens, pm, gm, lg, lb, wts, tables, row(ln1_g[0]), row(ln1_b[0]), cwb, ln2)
    return (_trunk(x_prompt, *args), _trunk(x_sample, *args))
```

```python
import functools

import numpy as np
import jax
import jax.numpy as jnp
from jax import lax
from jax.experimental import pallas as pl
from jax.experimental.pallas import tpu as pltpu

D_MODEL = 2048
N_META = 16
GRID_W = 64
NA_HEADS = 8
HEAD_DIM = 128
NA_WIN_ROWS = 8
NA_WIN_COLS = 16
SW_Q_HEADS = 8
SW_KV_HEADS = 2
SW_GROUP = SW_Q_HEADS // SW_KV_HEADS
SW_WINDOW = 128
SW_BLOCK = 128
D_FF = 5632
LN_EPS = 1e-5
NEG_INF = -1e30
DEPTH = 1
ALPHA = (2 * DEPTH) ** 0.25
SCALE = HEAD_DIM ** -0.5

LANE = 128
BF16_ROWS = 16
MXU_COLS = 256
VMEM_LIMIT = 56 * 1024 * 1024

SLAB_QA, SLAB_KA, SLAB_VA, SLAB_QB, SLAB_KB, SLAB_VB = 0, 8, 16, 24, 32, 34
QKV_COLS = 4608
QKV_SLABS = QKV_COLS // LANE
GATE_COLS = 2 * D_MODEL
GATE_SLABS = GATE_COLS // LANE
PROJ_CHUNK = 2 * MXU_COLS

NA_QROWS = 4
NA_WROWS = 11
NA_QBLK = NA_QROWS * GRID_W
NA_WKEYS = NA_WROWS * GRID_W
NA_KEYS = 768
SW_WKEYS = 3 * SW_BLOCK
SW_KEYS = 512
SW_MKEYS = 256
ATTN_UNROLL = 4

F32 = jnp.float32
BF16 = jnp.bfloat16


def _ln_rows(x, g, b):
    mu = jnp.mean(x, axis=-1, keepdims=True)
    xc = x - mu
    var = jnp.mean(xc * xc, axis=-1, keepdims=True)
    return xc * lax.rsqrt(var + LN_EPS) * g + b


def _scores(q, kcat, bias):
    s = lax.dot_general(q, kcat, (((1,), (1,)), ((), ())), preferred_element_type=F32)
    s = s * SCALE
    return s if bias is None else s + bias


def _softmax_pv_scores(s, vcat):
    m = jnp.max(s, axis=-1, keepdims=True)
    e = jnp.exp(s - m)
    l = jnp.sum(e, axis=-1, keepdims=True)
    o = jnp.dot(e.astype(BF16), vcat, preferred_element_type=F32)
    return o / l


def _softmax_pv(q, kcat, vcat, bias):
    return _softmax_pv_scores(_scores(q, kcat, bias), vcat)


def _pipelined_blocks(n_blocks, scores, finish):
    scores(jnp.int32(0), 0)

    def body(it, carry):
        for u in range(ATTN_UNROLL):
            i = it * ATTN_UNROLL + u
            scores(jnp.minimum(i + 1, n_blocks - 1), (u + 1) % ATTN_UNROLL)
            finish(i, u)
        return carry

    lax.fori_loop(0, n_blocks // ATTN_UNROLL, body, 0)


def _project_to_slabs(xn, w_ref, o_ref, rows):
    for c0 in range(0, w_ref.shape[1], PROJ_CHUNK):
        r = jnp.dot(xn, w_ref[:, c0:c0 + PROJ_CHUNK], preferred_element_type=F32)
        for s in range(PROJ_CHUNK // LANE):
            o_ref[c0 // LANE + s, rows, :] = r[:, s * LANE:(s + 1) * LANE].astype(BF16)


def _ln_gates_kernel(x_ref, g_ref, b_ref, w_ref, xn_ref, o_ref, *, parts):
    hr = x_ref.shape[0] // parts
    for r in range(parts):
        rows = slice(r * hr, (r + 1) * hr)
        xn = _ln_rows(x_ref[rows, :], g_ref[...], b_ref[...]).astype(BF16)
        xn_ref[rows, :] = xn
        _project_to_slabs(xn, w_ref, o_ref, rows)


def _ln_gates(x2, g, b, w, tm, parts):
    m = x2.shape[0]
    const = lambda shape: pl.BlockSpec(shape, lambda i: (0,) * len(shape), pipeline_mode=pl.Buffered(1))
    return pl.pallas_call(
        functools.partial(_ln_gates_kernel, parts=parts),
        out_shape=(jax.ShapeDtypeStruct((m, D_MODEL), BF16),
                   jax.ShapeDtypeStruct((GATE_SLABS, m, LANE), BF16)),
        grid=(m // tm,),
        in_specs=[pl.BlockSpec((tm, D_MODEL), lambda i: (i, 0)),
                  const((1, D_MODEL)), const((1, D_MODEL)), const(w.shape)],
        out_specs=(pl.BlockSpec((tm, D_MODEL), lambda i: (i, 0)),
                   pl.BlockSpec((GATE_SLABS, tm, LANE), lambda i: (0, i, 0))),
        compiler_params=pltpu.CompilerParams(dimension_semantics=("arbitrary",), vmem_limit_bytes=VMEM_LIMIT),
        name="ln_gates",
    )(x2, g, b, w)


def _qkv_kernel(xn_ref, w_ref, o_ref):
    _project_to_slabs(xn_ref[...], w_ref, o_ref, slice(None))


def _qkv_proj(xn, w, tm):
    m = xn.shape[0]
    return pl.pallas_call(
        _qkv_kernel,
        out_shape=jax.ShapeDtypeStruct((QKV_SLABS, m, LANE), BF16),
        grid=(m // tm,),
        in_specs=[pl.BlockSpec((tm, D_MODEL), lambda i: (i, 0)),
                  pl.BlockSpec(w.shape, lambda i: (0, 0), pipeline_mode=pl.Buffered(1))],
        out_specs=pl.BlockSpec((QKV_SLABS, tm, LANE), lambda i: (0, i, 0)),
        compiler_params=pltpu.CompilerParams(dimension_semantics=("arbitrary",), vmem_limit_bytes=VMEM_LIMIT),
        name="qkv_proj",
    )(xn, w)


def _na_kernel(q_ref, k_ref, v_ref, km_ref, vm_ref, bias_ref, o_ref, kcat_ref, vcat_ref, s_ref, *, rows, nblk):
    pad = jnp.zeros((NA_KEYS - NA_WKEYS - N_META, HEAD_DIM), BF16)
    for u in range(ATTN_UNROLL):
        kcat_ref[u, NA_WKEYS:NA_WKEYS + N_META, :] = km_ref[0]
        vcat_ref[u, NA_WKEYS:NA_WKEYS + N_META, :] = vm_ref[0]
        kcat_ref[u, NA_WKEYS + N_META:, :] = pad
        vcat_ref[u, NA_WKEYS + N_META:, :] = pad

    def window_start(i):
        r0 = i * NA_QROWS
        w0 = jnp.minimum(jnp.clip(r0 - NA_WIN_ROWS // 2, 0, rows - NA_WIN_ROWS), rows - NA_WROWS)
        return pl.multiple_of(w0 * GRID_W, GRID_W)

    def scores(i, slot):
        kcat_ref[slot, 0:NA_WKEYS, :] = k_ref[0, pl.ds(window_start(i), NA_WKEYS), :]
        variant = jnp.where(i == 0, 0, jnp.where(i == nblk - 1, 2, 1))
        q = q_ref[0, pl.ds(pl.multiple_of(i * NA_QBLK, NA_QBLK), NA_QBLK), :]
        s_ref[slot] = _scores(q, kcat_ref[slot], bias_ref[variant, 0])

    def finish(i, slot):
        vcat_ref[slot, 0:NA_WKEYS, :] = v_ref[0, pl.ds(window_start(i), NA_WKEYS), :]
        o = _softmax_pv_scores(s_ref[slot], vcat_ref[slot])
        o_ref[0, pl.ds(pl.multiple_of(i * NA_QBLK, NA_QBLK), NA_QBLK), :] = o.astype(BF16)

    _pipelined_blocks(nblk, scores, finish)


def _na_attn(p, pm, bias, b_sz, t):
    rows = t // GRID_W
    nblk = rows // NA_QROWS
    assert rows % NA_QROWS == 0 and nblk >= 3 and rows >= NA_WROWS and nblk % ATTN_UNROLL == 0
    seq = lambda slab: pl.BlockSpec((1, t, HEAD_DIM), lambda b, h: (slab + h, b, 0))
    meta = lambda slab: pl.BlockSpec((1, N_META, HEAD_DIM), lambda b, h: (slab + h, 0, 0))
    return pl.pallas_call(
        functools.partial(_na_kernel, rows=rows, nblk=nblk),
        out_shape=jax.ShapeDtypeStruct((NA_HEADS, b_sz * t, HEAD_DIM), BF16),
        grid=(b_sz, NA_HEADS),
        in_specs=[seq(SLAB_QA), seq(SLAB_KA), seq(SLAB_VA), meta(SLAB_KA), meta(SLAB_VA),
                  pl.BlockSpec((3, 1, NA_QBLK, NA_KEYS), lambda b, h: (0, h, 0, 0))],
        out_specs=pl.BlockSpec((1, t, HEAD_DIM), lambda b, h: (h, b, 0)),
        scratch_shapes=[pltpu.VMEM((ATTN_UNROLL, NA_KEYS, HEAD_DIM), BF16)] * 2
        + [pltpu.VMEM((ATTN_UNROLL, NA_QBLK, NA_KEYS), F32)],
        compiler_params=pltpu.CompilerParams(
            dimension_semantics=("arbitrary", "arbitrary"), vmem_limit_bytes=VMEM_LIMIT),
        name="na_attn",
    )(p, p, p, pm, pm, bias)


def _na_bias(rpb, dtype=F32):
    edge = GRID_W - NA_WIN_COLS
    ext = jnp.pad(rpb.astype(dtype), ((0, 0), (0, 0), (edge, edge)), mode="edge")
    cols = jnp.stack([ext[:, :, GRID_W - 1 - qc:2 * GRID_W - 1 - qc] for qc in range(GRID_W)], axis=2)
    rpad = NA_WROWS - NA_WIN_ROWS
    cols = jnp.pad(cols, ((0, 0), (rpad, rpad), (0, 0), (0, 0)))
    qc = np.arange(GRID_W)[:, None, None]
    j = np.arange(NA_WROWS)[None, :, None]
    kc = np.arange(GRID_W)[None, None, :]
    col_start = np.clip(qc - NA_WIN_COLS // 2, 0, GRID_W - NA_WIN_COLS)
    in_cols = (kc >= col_start) & (kc < col_start + NA_WIN_COLS)
    tables = []
    variants = ([(i, 0) for i in range(NA_QROWS)],
                [(i + NA_WIN_ROWS // 2, i) for i in range(NA_QROWS)],
                [(i + NA_WROWS - NA_QROWS, NA_WROWS - NA_WIN_ROWS) for i in range(NA_QROWS)])
    for variant in variants:
        per_row = []
        for qoff, rs in variant:
            r_lo = (NA_WIN_ROWS - 1) - qoff + rpad
            blk = jnp.transpose(cols[:, r_lo:r_lo + NA_WROWS], (0, 2, 1, 3))
            mask = ((j >= rs) & (j < rs + NA_WIN_ROWS)) & in_cols
            per_row.append(jnp.where(mask[None], blk, NEG_INF))
        tables.append(jnp.stack(per_row, axis=1).reshape(NA_HEADS, NA_QBLK, NA_WKEYS))
    tab = jnp.stack(tables)
    lead = tab.shape[:-1]
    return jnp.concatenate([tab, jnp.zeros(lead + (N_META,), dtype),
                            jnp.full(lead + (NA_KEYS - NA_WKEYS - N_META,), NEG_INF, dtype)], axis=-1)


def _swa_kernel(q_ref, k_ref, v_ref, km_ref, vm_ref, bias_ref, o_ref, kcat_ref, vcat_ref, s_ref, *, t, nb, bps):
    pad = jnp.zeros((SW_KEYS - SW_WKEYS - N_META, HEAD_DIM), BF16)
    for u in range(ATTN_UNROLL):
        kcat_ref[u, SW_WKEYS:SW_WKEYS + N_META, :] = km_ref[0]
        vcat_ref[u, SW_WKEYS:SW_WKEYS + N_META, :] = vm_ref[0]
        kcat_ref[u, SW_WKEYS + N_META:, :] = pad
        vcat_ref[u, SW_WKEYS + N_META:, :] = pad
    step = pl.program_id(2)

    def window_start(i):
        n = step * bps + i
        return pl.multiple_of(jnp.clip((n - 1) * SW_BLOCK, 0, t - SW_WKEYS), SW_BLOCK)

    def scores(i, slot):
        n = step * bps + i
        kcat_ref[slot, 0:SW_WKEYS, :] = k_ref[0, pl.ds(window_start(i), SW_WKEYS), :]
        variant = jnp.where(n == 0, 0, jnp.where(n == nb - 1, 2, 1))
        q = q_ref[:, pl.ds(pl.multiple_of(i * SW_BLOCK, SW_BLOCK), SW_BLOCK), :]
        bias = bias_ref[variant].reshape(SW_GROUP * SW_BLOCK, SW_KEYS)
        s_ref[slot] = _scores(q.reshape(SW_GROUP * SW_BLOCK, HEAD_DIM), kcat_ref[slot], bias)

    def finish(i, slot):
        vcat_ref[slot, 0:SW_WKEYS, :] = v_ref[0, pl.ds(window_start(i), SW_WKEYS), :]
        o = _softmax_pv_scores(s_ref[slot], vcat_ref[slot])
        qs = pl.multiple_of(i * SW_BLOCK, SW_BLOCK)
        o_ref[:, pl.ds(qs, SW_BLOCK), :] = o.reshape(SW_GROUP, SW_BLOCK, HEAD_DIM).astype(BF16)

    _pipelined_blocks(bps, scores, finish)


def _swa_attn(p, pm, bias, b_sz, t, tq):
    nb = t // SW_BLOCK
    assert nb >= 3 and t % tq == 0 and (tq // SW_BLOCK) % ATTN_UNROLL == 0
    steps = t // tq
    bps = tq // SW_BLOCK
    kv = lambda slab: pl.BlockSpec((1, t, HEAD_DIM), lambda b, g, s: (slab + g, b, 0))
    meta = lambda slab: pl.BlockSpec((1, N_META, HEAD_DIM), lambda b, g, s: (slab + g, 0, 0))
    return pl.pallas_call(
        functools.partial(_swa_kernel, t=t, nb=nb, bps=bps),
        out_shape=jax.ShapeDtypeStruct((SW_Q_HEADS, b_sz * t, HEAD_DIM), BF16),
        grid=(b_sz, SW_KV_HEADS, steps),
        in_specs=[pl.BlockSpec((SW_GROUP, tq, HEAD_DIM), lambda b, g, s: (SLAB_QB // SW_GROUP + g, b * steps + s, 0)),
                  kv(SLAB_KB), kv(SLAB_VB), meta(SLAB_KB), meta(SLAB_VB),
                  pl.BlockSpec((3, SW_GROUP, SW_BLOCK, SW_KEYS), lambda b, g, s: (0, g, 0, 0))],
        out_specs=pl.BlockSpec((SW_GROUP, tq, HEAD_DIM), lambda b, g, s: (g, b * steps + s, 0)),
        scratch_shapes=[pltpu.VMEM((ATTN_UNROLL, SW_KEYS, HEAD_DIM), BF16)] * 2
        + [pltpu.VMEM((ATTN_UNROLL, SW_GROUP * SW_BLOCK, SW_KEYS), F32)],
        compiler_params=pltpu.CompilerParams(
            dimension_semantics=("arbitrary", "arbitrary", "arbitrary"), vmem_limit_bytes=VMEM_LIMIT),
        name="swa_attn",
    )(p, p, p, pm, pm, bias)


def _sw_slopes():
    return np.power(2.0, -8.0 * np.arange(1, SW_Q_HEADS + 1, dtype=np.float64) / SW_Q_HEADS).astype(np.float32)


def _sw_bias(sink, dtype=F32):
    i = np.arange(SW_BLOCK)[:, None]
    j = np.arange(SW_WKEYS)[None, :]
    slopes = jnp.asarray(_sw_slopes(), dtype)[:, None, None]
    tables = []
    for qo in (0, SW_BLOCK, 2 * SW_BLOCK):
        dist = np.abs(qo + i - j)
        pen = -(jnp.asarray(dist, dtype)[None] * slopes)
        tables.append(jnp.where((dist <= SW_WINDOW)[None], pen, NEG_INF))
    tab = jnp.stack(tables)
    lead = tab.shape[:-1]
    snk = jnp.broadcast_to(sink.astype(dtype)[None, :, None, None], lead + (1,))
    return jnp.concatenate([tab, jnp.zeros(lead + (N_META,), dtype), snk,
                            jnp.full(lead + (SW_KEYS - SW_WKEYS - N_META - 1,), NEG_INF, dtype)], axis=-1)


def _sw_meta_bias(sink, dtype=F32):
    q = np.arange(N_META)[:, None]
    k = np.arange(SW_BLOCK)[None, :]
    dist = (N_META + k) - q
    slopes = jnp.asarray(_sw_slopes(), dtype)[:, None, None]
    pen = jnp.where((dist <= SW_WINDOW)[None], -(slopes * jnp.asarray(dist, dtype)[None]), NEG_INF)
    lead = pen.shape[:-1]
    snk = jnp.broadcast_to(sink.astype(dtype)[:, None, None], lead + (1,))
    return jnp.concatenate([jnp.zeros(lead + (N_META,), dtype), pen, snk,
                            jnp.full(lead + (SW_MKEYS - N_META - SW_BLOCK - 1,), NEG_INF, dtype)], axis=-1)


def _meta_attn_kernel(pm_ref, k0_ref, v0_ref, bias_ref, oa_ref, ob_ref, kcat_ref, vcat_ref):
    for h in range(NA_HEADS):
        o = _softmax_pv(pm_ref[SLAB_QA + h], pm_ref[SLAB_KA + h], pm_ref[SLAB_VA + h], None)
        oa_ref[h] = o.astype(BF16)
    pad = jnp.zeros((SW_MKEYS - N_META - SW_BLOCK, HEAD_DIM), BF16)
    for g in range(SW_KV_HEADS):
        kcat_ref[0:N_META, :] = pm_ref[SLAB_KB + g]
        vcat_ref[0:N_META, :] = pm_ref[SLAB_VB + g]
        kcat_ref[N_META:N_META + SW_BLOCK, :] = k0_ref[g]
        vcat_ref[N_META:N_META + SW_BLOCK, :] = v0_ref[g]
        kcat_ref[N_META + SW_BLOCK:, :] = pad
        vcat_ref[N_META + SW_BLOCK:, :] = pad
        q = jnp.concatenate([pm_ref[SLAB_QB + g * SW_GROUP + r] for r in range(SW_GROUP)], axis=0)
        bias = bias_ref[g * SW_GROUP:(g + 1) * SW_GROUP].reshape(SW_GROUP * N_META, SW_MKEYS)
        o = _softmax_pv(q, kcat_ref[...], vcat_ref[...], bias)
        for r in range(SW_GROUP):
            ob_ref[g * SW_GROUP + r] = o[r * N_META:(r + 1) * N_META].astype(BF16)


def _meta_attn(p, pm, bias, b_sz, t):
    blocks_per_seq = t // SW_BLOCK
    first_block = lambda slab: pl.BlockSpec((SW_KV_HEADS, SW_BLOCK, HEAD_DIM),
                                            lambda b: (slab // SW_KV_HEADS, b * blocks_per_seq, 0))
    out = jax.ShapeDtypeStruct((NA_HEADS, b_sz * N_META, HEAD_DIM), BF16)
    return pl.pallas_call(
        _meta_attn_kernel,
        out_shape=(out, out),
        grid=(b_sz,),
        in_specs=[pl.BlockSpec((QKV_SLABS, N_META, HEAD_DIM), lambda b: (0, 0, 0)),
                  first_block(SLAB_KB), first_block(SLAB_VB),
                  pl.BlockSpec((SW_Q_HEADS, N_META, SW_MKEYS), lambda b: (0, 0, 0))],
        out_specs=(pl.BlockSpec((NA_HEADS, N_META, HEAD_DIM), lambda b: (0, b, 0)),
                   pl.BlockSpec((SW_Q_HEADS, N_META, HEAD_DIM), lambda b: (0, b, 0))),
        scratch_shapes=[pltpu.VMEM((SW_MKEYS, HEAD_DIM), BF16), pltpu.VMEM((SW_MKEYS, HEAD_DIM), BF16)],
        compiler_params=pltpu.CompilerParams(dimension_semantics=("arbitrary",)),
        name="meta_attn",
    )(pm, p, p, bias)


def _post_kernel(x_ref, lg_ref, lb_ref, oa_ref, ob_ref, ga_ref, gb_ref, wna_ref, wsw_ref, wout_ref,
                 g1_ref, b1_ref, h_ref, *, parts):
    hr = x_ref.shape[0] // parts
    for r in range(parts):
        rows = slice(r * hr, (r + 1) * hr)
        oa = jnp.concatenate([oa_ref[h, rows, :] for h in range(NA_HEADS)], axis=1)
        ob = jnp.concatenate([ob_ref[h, rows, :] for h in range(SW_Q_HEADS)], axis=1)
        merged = []
        for c0 in range(0, D_MODEL, PROJ_CHUNK):
            cols = slice(c0, c0 + PROJ_CHUNK)
            slabs = range(c0 // LANE, (c0 + PROJ_CHUNK) // LANE)
            a = jnp.dot(oa, wna_ref[:, cols], preferred_element_type=F32)
            b = jnp.dot(ob, wsw_ref[:, cols], preferred_element_type=F32)
            ga = jnp.concatenate([ga_ref[s, rows, :] for s in slabs], axis=1).astype(F32)
            gb = jnp.concatenate([gb_ref[s, rows, :] for s in slabs], axis=1).astype(F32)
            merged.append((jax.nn.sigmoid(ga) * a + jax.nn.sigmoid(gb) * b).astype(BF16))
        merged = jnp.concatenate(merged, axis=1)
        y = jnp.dot(merged, wout_ref[...], preferred_element_type=F32)
        h0 = _ln_rows(x_ref[rows, :], lg_ref[...], lb_ref[...])
        h_ref[rows, :] = _ln_rows(ALPHA * h0 + y, g1_ref[...], b1_ref[...])


def _post_attn(x2, lg, lb, oa, ob, gates, wna, wsw, wout, g1, b1, tm, parts):
    m = x2.shape[0]
    n_g = D_MODEL // LANE
    const = lambda shape: pl.BlockSpec(shape, lambda i: (0,) * len(shape), pipeline_mode=pl.Buffered(1))
    heads = pl.BlockSpec((NA_HEADS, tm, HEAD_DIM), lambda i: (0, i, 0))
    return pl.pallas_call(
        functools.partial(_post_kernel, parts=parts),
        out_shape=jax.ShapeDtypeStruct((m, D_MODEL), F32),
        grid=(m // tm,),
        in_specs=[pl.BlockSpec((tm, D_MODEL), lambda i: (i, 0)),
                  const((1, D_MODEL)), const((1, D_MODEL)),
                  heads, heads,
                  pl.BlockSpec((n_g, tm, LANE), lambda i: (0, i, 0)),
                  pl.BlockSpec((n_g, tm, LANE), lambda i: (1, i, 0)),
                  const(wna.shape), const(wsw.shape), const(wout.shape),
                  const((1, D_MODEL)), const((1, D_MODEL))],
        out_specs=pl.BlockSpec((tm, D_MODEL), lambda i: (i, 0)),
        compiler_params=pltpu.CompilerParams(dimension_semantics=("arbitrary",), vmem_limit_bytes=VMEM_LIMIT),
        name="post_attn",
    )(x2, lg, lb, oa, ob, gates, gates, wna, wsw, wout, g1, b1)


HALO = BF16_ROWS
FFN_SUB = MXU_COLS


def _ffn_kernel(h_ref, hn_ref, hm_ref, wg_ref, wv_ref, cwb_ref, wd_ref, ln_ref, o_ref, hb_ref, *, tm, tf, tiles_per_seq):
    i = pl.program_id(0)
    f = pl.program_id(1)
    nf = pl.num_programs(1)
    pos = i % tiles_per_seq

    @pl.when(jnp.logical_and(f == 0, pos == 0))
    def _():
        hb_ref[0:HALO, :] = hm_ref[...].astype(BF16)

    @pl.when(jnp.logical_and(f == 0, pos != 0))
    def _():
        hb_ref[0:HALO, :] = hb_ref[tm:tm + HALO, :]

    @pl.when(f == 0)
    def _():
        nxt = jnp.where(pos == tiles_per_seq - 1, 0.0, hn_ref[...])
        hb_ref[HALO:HALO + tm, :] = h_ref[...].astype(BF16)
        hb_ref[HALO + tm:, :] = nxt.astype(BF16)
        o_ref[...] = jnp.zeros_like(o_ref)

    n = tm + 2 * HALO
    part = None
    for c0 in range(0, tf, FFN_SUB):
        sl = slice(c0, c0 + FFN_SUB)
        gp = jnp.dot(hb_ref[...], wg_ref[:, sl], preferred_element_type=F32)
        vl = jnp.dot(hb_ref[HALO:HALO + tm, :], wv_ref[:, sl], preferred_element_type=F32)
        up = pltpu.roll(gp, 1, 0)
        dn = pltpu.roll(gp, n - 1, 0)
        gate = (up * cwb_ref[0:1, sl] + gp * cwb_ref[1:2, sl] + dn * cwb_ref[2:3, sl])[HALO:HALO + tm] + cwb_ref[3:4, sl]
        act = (jax.nn.gelu(gate) * vl).astype(BF16)
        d = jnp.dot(act, wd_ref[sl, :], preferred_element_type=F32)
        part = d if part is None else part + d
    o_ref[...] += part

    @pl.when(f == nf - 1)
    def _():
        o_ref[...] = _ln_rows(ALPHA * h_ref[...] + o_ref[...], ln_ref[0:1, :], ln_ref[1:2, :])


def _ffn(h, h_meta, w_in, cwb, wd, ln2, t, tm, tf):
    m = h.shape[0]
    nf = D_FF // tf
    tiles_per_seq = t // tm
    hb = tm // HALO
    last_hb = m // HALO - 1
    return pl.pallas_call(
        functools.partial(_ffn_kernel, tm=tm, tf=tf, tiles_per_seq=tiles_per_seq),
        out_shape=jax.ShapeDtypeStruct((m, D_MODEL), F32),
        grid=(m // tm, nf),
        in_specs=[pl.BlockSpec((tm, D_MODEL), lambda i, f: (i, 0)),
                  pl.BlockSpec((HALO, D_MODEL), lambda i, f: (jnp.minimum((i + 1) * hb, last_hb), 0)),
                  pl.BlockSpec((N_META, D_MODEL), lambda i, f: (i // tiles_per_seq, 0)),
                  pl.BlockSpec((D_MODEL, tf), lambda i, f: (0, f)),
                  pl.BlockSpec((D_MODEL, tf), lambda i, f: (0, f + nf)),
                  pl.BlockSpec((4, tf), lambda i, f: (0, f)),
                  pl.BlockSpec((tf, D_MODEL), lambda i, f: (f, 0)),
                  pl.BlockSpec((2, D_MODEL), lambda i, f: (0, 0))],
        out_specs=pl.BlockSpec((tm, D_MODEL), lambda i, f: (i, 0)),
        scratch_shapes=[pltpu.VMEM((tm + 2 * HALO, D_MODEL), BF16)],
        compiler_params=pltpu.CompilerParams(
            dimension_semantics=("arbitrary", "arbitrary"), vmem_limit_bytes=VMEM_LIMIT),
        name="ffn",
    )(h, h, h_meta, w_in, w_in, cwb, wd, ln2)


def _layer_weights(w_in, w_proj_na, w_proj_sw, w_out, w_ffn_in, w_ffn_down):
    return (w_in[:, QKV_COLS:].astype(BF16), w_in[:, :QKV_COLS].astype(BF16),
            w_proj_na.astype(BF16), w_proj_sw.astype(BF16), w_out.astype(BF16),
            w_ffn_in.astype(BF16), w_ffn_down.astype(BF16))


def _trunk(x, meta_tokens, pm, gm, lg, lb, wts, tables, g1, b1, cwb, ln2):
    b_sz, t, _ = x.shape
    wgate, wqkv, wna, wsw, wout, wffn, wdown = wts
    na_bias, sw_bias, swm_bias = tables
    x2 = x.reshape(b_sz * t, D_MODEL)
    xn, gates = _ln_gates(x2, lg, lb, wgate, tm=512, parts=2)
    p = _qkv_proj(xn, wqkv, tm=1024)
    oa = _na_attn(p, pm, na_bias, b_sz, t)
    ob = _swa_attn(p, pm, sw_bias, b_sz, t, tq=2048)
    oa_m, ob_m = _meta_attn(p, pm, swm_bias, b_sz, t)
    h = _post_attn(x2, lg, lb, oa, ob, gates, wna, wsw, wout, g1, b1, tm=512, parts=2)
    x_m = jnp.tile(meta_tokens, (b_sz, 1))
    gates_m = jnp.tile(gm, (1, b_sz, 1))
    h_m = _post_attn(x_m, lg, lb, oa_m, ob_m, gates_m, wna, wsw, wout, g1, b1, tm=b_sz * N_META, parts=1)
    y = _ffn(h, h_m, wffn, cwb, wdown, ln2, t, tm=512, tf=512)
    return y.reshape(b_sz, t, D_MODEL)


def kernel(x_prompt, x_sample, meta_tokens, ln_emb_g, ln_emb_b, w_in, na_rpb, sw_sink, w_proj_na, w_proj_sw, w_out, ln1_g, ln1_b, w_ffn_in, ffn_conv_w, ffn_conv_b, w_ffn_down, ln2_g, ln2_b):
    assert DEPTH == 1 and w_in.shape[0] == DEPTH
    row = lambda v: v.reshape(1, -1)
    lg, lb = row(ln_emb_g), row(ln_emb_b)
    wts = _layer_weights(w_in[0], w_proj_na[0], w_proj_sw[0], w_out[0], w_ffn_in[0], w_ffn_down[0])
    tables = (_na_bias(na_rpb[0]), _sw_bias(sw_sink[0]), _sw_meta_bias(sw_sink[0]))
    xn_m, gm = _ln_gates(meta_tokens, lg, lb, wts[0], tm=N_META, parts=1)
    pm = _qkv_proj(xn_m, wts[1], tm=N_META)
    cwb = jnp.concatenate([ffn_conv_w[0], row(ffn_conv_b[0])], axis=0)
    ln2 = jnp.stack([ln2_g[0], ln2_b[0]])
    args = (meta_tokens, pm, gm, lg, lb, wts, tables, row(ln1_g[0]), row(ln1_b[0]), cwb, ln2)
    return (_trunk(x_prompt, *args), _trunk(x_sample, *args))
```

```python
import functools

import numpy as np
import jax
import jax.numpy as jnp
from jax import lax
from jax.experimental import pallas as pl
from jax.experimental.pallas import tpu as pltpu

D_MODEL = 2048
N_META = 16
GRID_W = 64
NA_HEADS = 8
HEAD_DIM = 128
NA_WIN_ROWS = 8
NA_WIN_COLS = 16
SW_Q_HEADS = 8
SW_KV_HEADS = 2
SW_GROUP = SW_Q_HEADS // SW_KV_HEADS
SW_WINDOW = 128
SW_BLOCK = 128
D_FF = 5632
LN_EPS = 1e-5
NEG_INF = -1e30
DEPTH = 1
ALPHA = (2 * DEPTH) ** 0.25
SCALE = HEAD_DIM ** -0.5

LANE = 128
BF16_ROWS = 16
MXU_COLS = 256
VMEM_LIMIT = 56 * 1024 * 1024

SLAB_QA, SLAB_KA, SLAB_VA, SLAB_QB, SLAB_KB, SLAB_VB = 0, 8, 16, 24, 32, 34
QKV_COLS = 4608
QKV_SLABS = QKV_COLS // LANE
GATE_COLS = 2 * D_MODEL
GATE_SLABS = GATE_COLS // LANE
PROJ_CHUNK = 2 * MXU_COLS

NA_QROWS = 4
NA_WROWS = 11
NA_QBLK = NA_QROWS * GRID_W
NA_WKEYS = NA_WROWS * GRID_W
NA_KEYS = 768
SW_WKEYS = 3 * SW_BLOCK
SW_KEYS = 512
SW_MKEYS = 256
ATTN_UNROLL = 4

F32 = jnp.float32
BF16 = jnp.bfloat16


def _ln_rows(x, g, b):
    mu = jnp.mean(x, axis=-1, keepdims=True)
    xc = x - mu
    var = jnp.mean(xc * xc, axis=-1, keepdims=True)
    return xc * lax.rsqrt(var + LN_EPS) * g + b


def _scores(q, kcat, bias):
    s = lax.dot_general(q, kcat, (((1,), (1,)), ((), ())), preferred_element_type=F32)
    s = s * SCALE
    return s if bias is None else s + bias


def _softmax_pv_scores(s, vcat):
    m = jnp.max(s, axis=-1, keepdims=True)
    e = jnp.exp(s - m)
    l = jnp.sum(e, axis=-1, keepdims=True)
    o = jnp.dot(e.astype(BF16), vcat, preferred_element_type=F32)
    return o / l


def _softmax_pv(q, kcat, vcat, bias):
    return _softmax_pv_scores(_scores(q, kcat, bias), vcat)


def _pipelined_blocks(n_blocks, scores, finish):
    scores(jnp.int32(0), 0)

    def body(it, carry):
        for u in range(ATTN_UNROLL):
            i = it * ATTN_UNROLL + u
            scores(jnp.minimum(i + 1, n_blocks - 1), (u + 1) % ATTN_UNROLL)
            finish(i, u)
        return carry

    lax.fori_loop(0, n_blocks // ATTN_UNROLL, body, 0)


def _project_to_slabs(xn, w_ref, o_ref, rows):
    for c0 in range(0, w_ref.shape[1], PROJ_CHUNK):
        r = jnp.dot(xn, w_ref[:, c0:c0 + PROJ_CHUNK], preferred_element_type=F32)
        for s in range(PROJ_CHUNK // LANE):
            o_ref[c0 // LANE + s, rows, :] = r[:, s * LANE:(s + 1) * LANE].astype(BF16)


def _ln_gates_kernel(x_ref, g_ref, b_ref, w_ref, xn_ref, o_ref, *, parts):
    hr = x_ref.shape[0] // parts
    for r in range(parts):
        rows = slice(r * hr, (r + 1) * hr)
        xn = _ln_rows(x_ref[rows, :], g_ref[...], b_ref[...]).astype(BF16)
        xn_ref[rows, :] = xn
        _project_to_slabs(xn, w_ref, o_ref, rows)


def _ln_gates(x2, g, b, w, tm, parts):
    m = x2.shape[0]
    const = lambda shape: pl.BlockSpec(shape, lambda i: (0,) * len(shape), pipeline_mode=pl.Buffered(1))
    return pl.pallas_call(
        functools.partial(_ln_gates_kernel, parts=parts),
        out_shape=(jax.ShapeDtypeStruct((m, D_MODEL), BF16),
                   jax.ShapeDtypeStruct((GATE_SLABS, m, LANE), BF16)),
        grid=(m // tm,),
        in_specs=[pl.BlockSpec((tm, D_MODEL), lambda i: (i, 0)),
                  const((1, D_MODEL)), const((1, D_MODEL)), const(w.shape)],
        out_specs=(pl.BlockSpec((tm, D_MODEL), lambda i: (i, 0)),
                   pl.BlockSpec((GATE_SLABS, tm, LANE), lambda i: (0, i, 0))),
        compiler_params=pltpu.CompilerParams(dimension_semantics=("arbitrary",), vmem_limit_bytes=VMEM_LIMIT),
        name="ln_gates",
    )(x2, g, b, w)


def _qkv_kernel(xn_ref, w_ref, o_ref):
    _project_to_slabs(xn_ref[...], w_ref, o_ref, slice(None))


def _qkv_proj(xn, w, tm):
    m = xn.shape[0]
    return pl.pallas_call(
        _qkv_kernel,
        out_shape=jax.ShapeDtypeStruct((QKV_SLABS, m, LANE), BF16),
        grid=(m // tm,),
        in_specs=[pl.BlockSpec((tm, D_MODEL), lambda i: (i, 0)),
                  pl.BlockSpec(w.shape, lambda i: (0, 0), pipeline_mode=pl.Buffered(1))],
        out_specs=pl.BlockSpec((QKV_SLABS, tm, LANE), lambda i: (0, i, 0)),
        compiler_params=pltpu.CompilerParams(dimension_semantics=("arbitrary",), vmem_limit_bytes=VMEM_LIMIT),
        name="qkv_proj",
    )(xn, w)


def _na_kernel(q_ref, k_ref, v_ref, km_ref, vm_ref, bias_ref, o_ref, kcat_ref, vcat_ref, s_ref, *, rows, nblk):
    pad = jnp.zeros((NA_KEYS - NA_WKEYS - N_META, HEAD_DIM), BF16)
    for u in range(ATTN_UNROLL):
        kcat_ref[u, NA_WKEYS:NA_WKEYS + N_META, :] = km_ref[0]
        vcat_ref[u, NA_WKEYS:NA_WKEYS + N_META, :] = vm_ref[0]
        kcat_ref[u, NA_WKEYS + N_META:, :] = pad
        vcat_ref[u, NA_WKEYS + N_META:, :] = pad

    def window_start(i):
        r0 = i * NA_QROWS
        w0 = jnp.minimum(jnp.clip(r0 - NA_WIN_ROWS // 2, 0, rows - NA_WIN_ROWS), rows - NA_WROWS)
        return pl.multiple_of(w0 * GRID_W, GRID_W)

    def scores(i, slot):
        kcat_ref[slot, 0:NA_WKEYS, :] = k_ref[0, pl.ds(window_start(i), NA_WKEYS), :]
        variant = jnp.where(i == 0, 0, jnp.where(i == nblk - 1, 2, 1))
        q = q_ref[0, pl.ds(pl.multiple_of(i * NA_QBLK, NA_QBLK), NA_QBLK), :]
        s_ref[slot] = _scores(q, kcat_ref[slot], bias_ref[variant, 0])

    def finish(i, slot):
        vcat_ref[slot, 0:NA_WKEYS, :] = v_ref[0, pl.ds(window_start(i), NA_WKEYS), :]
        o = _softmax_pv_scores(s_ref[slot], vcat_ref[slot])
        o_ref[0, pl.ds(pl.multiple_of(i * NA_QBLK, NA_QBLK), NA_QBLK), :] = o.astype(BF16)

    _pipelined_blocks(nblk, scores, finish)


def _na_attn(p, pm, bias, b_sz, t):
    rows = t // GRID_W
    nblk = rows // NA_QROWS
    assert rows % NA_QROWS == 0 and nblk >= 3 and rows >= NA_WROWS and nblk % ATTN_UNROLL == 0
    seq = lambda slab: pl.BlockSpec((1, t, HEAD_DIM), lambda b, h: (slab + h, b, 0))
    meta = lambda slab: pl.BlockSpec((1, N_META, HEAD_DIM), lambda b, h: (slab + h, 0, 0))
    return pl.pallas_call(
        functools.partial(_na_kernel, rows=rows, nblk=nblk),
        out_shape=jax.ShapeDtypeStruct((NA_HEADS, b_sz * t, HEAD_DIM), BF16),
        grid=(b_sz, NA_HEADS),
        in_specs=[seq(SLAB_QA), seq(SLAB_KA), seq(SLAB_VA), meta(SLAB_KA), meta(SLAB_VA),
                  pl.BlockSpec((3, 1, NA_QBLK, NA_KEYS), lambda b, h: (0, h, 0, 0))],
        out_specs=pl.BlockSpec((1, t, HEAD_DIM), lambda b, h: (h, b, 0)),
        scratch_shapes=[pltpu.VMEM((ATTN_UNROLL, NA_KEYS, HEAD_DIM), BF16)] * 2
        + [pltpu.VMEM((ATTN_UNROLL, NA_QBLK, NA_KEYS), F32)],
        compiler_params=pltpu.CompilerParams(
            dimension_semantics=("arbitrary", "arbitrary"), vmem_limit_bytes=VMEM_LIMIT),
        name="na_attn",
    )(p, p, p, pm, pm, bias)


def _na_bias(rpb, dtype=F32):
    edge = GRID_W - NA_WIN_COLS
    ext = jnp.pad(rpb.astype(dtype), ((0, 0), (0, 0), (edge, edge)), mode="edge")
    cols = jnp.stack([ext[:, :, GRID_W - 1 - qc:2 * GRID_W - 1 - qc] for qc in range(GRID_W)], axis=2)
    rpad = NA_WROWS - NA_WIN_ROWS
    cols = jnp.pad(cols, ((0, 0), (rpad, rpad), (0, 0), (0, 0)))
    qc = np.arange(GRID_W)[:, None, None]
    j = np.arange(NA_WROWS)[None, :, None]
    kc = np.arange(GRID_W)[None, None, :]
    col_start = np.clip(qc - NA_WIN_COLS // 2, 0, GRID_W - NA_WIN_COLS)
    in_cols = (kc >= col_start) & (kc < col_start + NA_WIN_COLS)
    tables = []
    variants = ([(i, 0) for i in range(NA_QROWS)],
                [(i + NA_WIN_ROWS // 2, i) for i in range(NA_QROWS)],
                [(i + NA_WROWS - NA_QROWS, NA_WROWS - NA_WIN_ROWS) for i in range(NA_QROWS)])
    for variant in variants:
        per_row = []
        for qoff, rs in variant:
            r_lo = (NA_WIN_ROWS - 1) - qoff + rpad
            blk = jnp.transpose(cols[:, r_lo:r_lo + NA_WROWS], (0, 2, 1, 3))
            mask = ((j >= rs) & (j < rs + NA_WIN_ROWS)) & in_cols
            per_row.append(jnp.where(mask[None], blk, NEG_INF))
        tables.append(jnp.stack(per_row, axis=1).reshape(NA_HEADS, NA_QBLK, NA_WKEYS))
    tab = jnp.stack(tables)
    lead = tab.shape[:-1]
    return jnp.concatenate([tab, jnp.zeros(lead + (N_META,), dtype),
                            jnp.full(lead + (NA_KEYS - NA_WKEYS - N_META,), NEG_INF, dtype)], axis=-1)


def _swa_kernel(q_ref, k_ref, v_ref, km_ref, vm_ref, bias_ref, o_ref, kcat_ref, vcat_ref, s_ref, *, t, nb, bps):
    pad = jnp.zeros((SW_KEYS - SW_WKEYS - N_META, HEAD_DIM), BF16)
    for u in range(ATTN_UNROLL):
        kcat_ref[u, SW_WKEYS:SW_WKEYS + N_META, :] = km_ref[0]
        vcat_ref[u, SW_WKEYS:SW_WKEYS + N_META, :] = vm_ref[0]
        kcat_ref[u, SW_WKEYS + N_META:, :] = pad
        vcat_ref[u, SW_WKEYS + N_META:, :] = pad
    step = pl.program_id(2)

    def window_start(i):
        n = step * bps + i
        return pl.multiple_of(jnp.clip((n - 1) * SW_BLOCK, 0, t - SW_WKEYS), SW_BLOCK)

    def scores(i, slot):
        n = step * bps + i
        kcat_ref[slot, 0:SW_WKEYS, :] = k_ref[0, pl.ds(window_start(i), SW_WKEYS), :]
        variant = jnp.where(n == 0, 0, jnp.where(n == nb - 1, 2, 1))
        q = q_ref[:, pl.ds(pl.multiple_of(i * SW_BLOCK, SW_BLOCK), SW_BLOCK), :]
        bias = bias_ref[variant].reshape(SW_GROUP * SW_BLOCK, SW_KEYS)
        s_ref[slot] = _scores(q.reshape(SW_GROUP * SW_BLOCK, HEAD_DIM), kcat_ref[slot], bias)

    def finish(i, slot):
        vcat_ref[slot, 0:SW_WKEYS, :] = v_ref[0, pl.ds(window_start(i), SW_WKEYS), :]
        o = _softmax_pv_scores(s_ref[slot], vcat_ref[slot])
        qs = pl.multiple_of(i * SW_BLOCK, SW_BLOCK)
        o_ref[:, pl.ds(qs, SW_BLOCK), :] = o.reshape(SW_GROUP, SW_BLOCK, HEAD_DIM).astype(BF16)

    _pipelined_blocks(bps, scores, finish)


def _swa_attn(p, pm, bias, b_sz, t, tq):
    nb = t // SW_BLOCK
    assert nb >= 3 and t % tq == 0 and (tq // SW_BLOCK) % ATTN_UNROLL == 0
    steps = t // tq
    bps = tq // SW_BLOCK
    kv = lambda slab: pl.BlockSpec((1, t, HEAD_DIM), lambda b, g, s: (slab + g, b, 0))
    meta = lambda slab: pl.BlockSpec((1, N_META, HEAD_DIM), lambda b, g, s: (slab + g, 0, 0))
    return pl.pallas_call(
        functools.partial(_swa_kernel, t=t, nb=nb, bps=bps),
        out_shape=jax.ShapeDtypeStruct((SW_Q_HEADS, b_sz * t, HEAD_DIM), BF16),
        grid=(b_sz, SW_KV_HEADS, steps),
        in_specs=[pl.BlockSpec((SW_GROUP, tq, HEAD_DIM), lambda b, g, s: (SLAB_QB // SW_GROUP + g, b * steps + s, 0)),
                  kv(SLAB_KB), kv(SLAB_VB), meta(SLAB_KB), meta(SLAB_VB),
                  pl.BlockSpec((3, SW_GROUP, SW_BLOCK, SW_KEYS), lambda b, g, s: (0, g, 0, 0))],
        out_specs=pl.BlockSpec((SW_GROUP, tq, HEAD_DIM), lambda b, g, s: (g, b * steps + s, 0)),
        scratch_shapes=[pltpu.VMEM((ATTN_UNROLL, SW_KEYS, HEAD_DIM), BF16)] * 2
        + [pltpu.VMEM((ATTN_UNROLL, SW_GROUP * SW_BLOCK, SW_KEYS), F32)],
        compiler_params=pltpu.CompilerParams(
            dimension_semantics=("arbitrary", "arbitrary", "arbitrary"), vmem_limit_bytes=VMEM_LIMIT),
        name="swa_attn",
    )(p, p, p, pm, pm, bias)


def _sw_slopes():
    return np.power(2.0, -8.0 * np.arange(1, SW_Q_HEADS + 1, dtype=np.float64) / SW_Q_HEADS).astype(np.float32)


def _sw_bias(sink, dtype=F32):
    i = np.arange(SW_BLOCK)[:, None]
    j = np.arange(SW_WKEYS)[None, :]
    slopes = jnp.asarray(_sw_slopes(), dtype)[:, None, None]
    tables = []
    for qo in (0, SW_BLOCK, 2 * SW_BLOCK):
        dist = np.abs(qo + i - j)
        pen = -(jnp.asarray(dist, dtype)[None] * slopes)
        tables.append(jnp.where((dist <= SW_WINDOW)[None], pen, NEG_INF))
    tab = jnp.stack(tables)
    lead = tab.shape[:-1]
    snk = jnp.broadcast_to(sink.astype(dtype)[None, :, None, None], lead + (1,))
    return jnp.concatenate([tab, jnp.zeros(lead + (N_META,), dtype), snk,
                            jnp.full(lead + (SW_KEYS - SW_WKEYS - N_META - 1,), NEG_INF, dtype)], axis=-1)


def _sw_meta_bias(sink, dtype=F32):
    q = np.arange(N_META)[:, None]
    k = np.arange(SW_BLOCK)[None, :]
    dist = (N_META + k) - q
    slopes = jnp.asarray(_sw_slopes(), dtype)[:, None, None]
    pen = jnp.where((dist <= SW_WINDOW)[None], -(slopes * jnp.asarray(dist, dtype)[None]), NEG_INF)
    lead = pen.shape[:-1]
    snk = jnp.broadcast_to(sink.astype(dtype)[:, None, None], lead + (1,))
    return jnp.concatenate([jnp.zeros(lead + (N_META,), dtype), pen, snk,
                            jnp.full(lead + (SW_MKEYS - N_META - SW_BLOCK - 1,), NEG_INF, dtype)], axis=-1)


def _meta_attn_kernel(pm_ref, k0_ref, v0_ref, bias_ref, oa_ref, ob_ref, kcat_ref, vcat_ref):
    for h in range(NA_HEADS):
        o = _softmax_pv(pm_ref[SLAB_QA + h], pm_ref[SLAB_KA + h], pm_ref[SLAB_VA + h], None)
        oa_ref[h] = o.astype(BF16)
    pad = jnp.zeros((SW_MKEYS - N_META - SW_BLOCK, HEAD_DIM), BF16)
    for g in range(SW_KV_HEADS):
        kcat_ref[0:N_META, :] = pm_ref[SLAB_KB + g]
        vcat_ref[0:N_META, :] = pm_ref[SLAB_VB + g]
        kcat_ref[N_META:N_META + SW_BLOCK, :] = k0_ref[g]
        vcat_ref[N_META:N_META + SW_BLOCK, :] = v0_ref[g]
        kcat_ref[N_META + SW_BLOCK:, :] = pad
        vcat_ref[N_META + SW_BLOCK:, :] = pad
        q = jnp.concatenate([pm_ref[SLAB_QB + g * SW_GROUP + r] for r in range(SW_GROUP)], axis=0)
        bias = bias_ref[g * SW_GROUP:(g + 1) * SW_GROUP].reshape(SW_GROUP * N_META, SW_MKEYS)
        o = _softmax_pv(q, kcat_ref[...], vcat_ref[...], bias)
        for r in range(SW_GROUP):
            ob_ref[g * SW_GROUP + r] = o[r * N_META:(r + 1) * N_META].astype(BF16)


def _meta_attn(p, pm, bias, b_sz, t):
    blocks_per_seq = t // SW_BLOCK
    first_block = lambda slab: pl.BlockSpec((SW_KV_HEADS, SW_BLOCK, HEAD_DIM),
                                            lambda b: (slab // SW_KV_HEADS, b * blocks_per_seq, 0))
    out = jax.ShapeDtypeStruct((NA_HEADS, b_sz * N_META, HEAD_DIM), BF16)
    return pl.pallas_call(
        _meta_attn_kernel,
        out_shape=(out, out),
        grid=(b_sz,),
        in_specs=[pl.BlockSpec((QKV_SLABS, N_META, HEAD_DIM), lambda b: (0, 0, 0)),
                  first_block(SLAB_KB), first_block(SLAB_VB),
                  pl.BlockSpec((SW_Q_HEADS, N_META, SW_MKEYS), lambda b: (0, 0, 0))],
        out_specs=(pl.BlockSpec((NA_HEADS, N_META, HEAD_DIM), lambda b: (0, b, 0)),
                   pl.BlockSpec((SW_Q_HEADS, N_META, HEAD_DIM), lambda b: (0, b, 0))),
        scratch_shapes=[pltpu.VMEM((SW_MKEYS, HEAD_DIM), BF16), pltpu.VMEM((SW_MKEYS, HEAD_DIM), BF16)],
        compiler_params=pltpu.CompilerParams(dimension_semantics=("arbitrary",)),
        name="meta_attn",
    )(pm, p, p, bias)


def _post_kernel(x_ref, lg_ref, lb_ref, oa_ref, ob_ref, ga_ref, gb_ref, wna_ref, wsw_ref, wout_ref,
                 g1_ref, b1_ref, h_ref, *, parts):
    hr = x_ref.shape[0] // parts
    for r in range(parts):
        rows = slice(r * hr, (r + 1) * hr)
        oa = jnp.concatenate([oa_ref[h, rows, :] for h in range(NA_HEADS)], axis=1)
        ob = jnp.concatenate([ob_ref[h, rows, :] for h in range(SW_Q_HEADS)], axis=1)
        merged = []
        for c0 in range(0, D_MODEL, PROJ_CHUNK):
            cols = slice(c0, c0 + PROJ_CHUNK)
            slabs = range(c0 // LANE, (c0 + PROJ_CHUNK) // LANE)
            a = jnp.dot(oa, wna_ref[:, cols], preferred_element_type=F32)
            b = jnp.dot(ob, wsw_ref[:, cols], preferred_element_type=F32)
            ga = jnp.concatenate([ga_ref[s, rows, :] for s in slabs], axis=1).astype(F32)
            gb = jnp.concatenate([gb_ref[s, rows, :] for s in slabs], axis=1).astype(F32)
            merged.append((jax.nn.sigmoid(ga) * a + jax.nn.sigmoid(gb) * b).astype(BF16))
        merged = jnp.concatenate(merged, axis=1)
        y = jnp.dot(merged, wout_ref[...], preferred_element_type=F32)
        h0 = _ln_rows(x_ref[rows, :], lg_ref[...], lb_ref[...])
        h_ref[rows, :] = _ln_rows(ALPHA * h0 + y, g1_ref[...], b1_ref[...])


def _post_attn(x2, lg, lb, oa, ob, gates, wna, wsw, wout, g1, b1, tm, parts):
    m = x2.shape[0]
    n_g = D_MODEL // LANE
    const = lambda shape: pl.BlockSpec(shape, lambda i: (0,) * len(shape), pipeline_mode=pl.Buffered(1))
    heads = pl.BlockSpec((NA_HEADS, tm, HEAD_DIM), lambda i: (0, i, 0))
    return pl.pallas_call(
        functools.partial(_post_kernel, parts=parts),
        out_shape=jax.ShapeDtypeStruct((m, D_MODEL), F32),
        grid=(m // tm,),
        in_specs=[pl.BlockSpec((tm, D_MODEL), lambda i: (i, 0)),
                  const((1, D_MODEL)), const((1, D_MODEL)),
                  heads, heads,
                  pl.BlockSpec((n_g, tm, LANE), lambda i: (0, i, 0)),
                  pl.BlockSpec((n_g, tm, LANE), lambda i: (1, i, 0)),
                  const(wna.shape), const(wsw.shape), const(wout.shape),
                  const((1, D_MODEL)), const((1, D_MODEL))],
        out_specs=pl.BlockSpec((tm, D_MODEL), lambda i: (i, 0)),
        compiler_params=pltpu.CompilerParams(dimension_semantics=("arbitrary",), vmem_limit_bytes=VMEM_LIMIT),
        name="post_attn",
    )(x2, lg, lb, oa, ob, gates, gates, wna, wsw, wout, g1, b1)


HALO = BF16_ROWS
FFN_SUB = MXU_COLS


_GELU_C = 2.0 * float(np.sqrt(2.0 / np.pi)) * float(np.log2(np.e))


def _gelu_tanh(x):
    u = x * (x * x * (-_GELU_C * 0.044715) - _GELU_C)
    return x / (1.0 + jnp.exp2(u))


def _ffn_kernel(h_ref, hn_ref, hm_ref, wg_ref, wv_ref, cwb_ref, wd_ref, ln_ref, o_ref, hb_ref, act_a, act_b, gp_ref,
                *, tm, tf, tiles_per_seq):
    i = pl.program_id(0)
    f = pl.program_id(1)
    nf = pl.num_programs(1) - 1
    pos = i % tiles_per_seq

    @pl.when(jnp.logical_and(f == 0, pos == 0))
    def _():
        hb_ref[0:HALO, :] = hm_ref[...].astype(BF16)

    @pl.when(jnp.logical_and(f == 0, pos != 0))
    def _():
        hb_ref[0:HALO, :] = hb_ref[tm:tm + HALO, :]

    @pl.when(f == 0)
    def _():
        nxt = jnp.where(pos == tiles_per_seq - 1, 0.0, hn_ref[...])
        hb_ref[HALO:HALO + tm, :] = h_ref[...].astype(BF16)
        hb_ref[HALO + tm:, :] = nxt.astype(BF16)
        o_ref[...] = jnp.zeros_like(o_ref)

    def up_stage(act_out):
        n = tm + 2 * HALO
        for ci, c0 in enumerate(range(0, tf, FFN_SUB)):
            sl = slice(c0, c0 + FFN_SUB)
            gp = jnp.concatenate([jnp.dot(hb_ref[r0:r0 + n // 2, :], wg_ref[:, sl], preferred_element_type=F32)
                                  for r0 in (0, n // 2)], axis=0)
            vl = jnp.concatenate([jnp.dot(hb_ref[HALO + r0:HALO + r0 + tm // 2, :], wv_ref[:, sl],
                                          preferred_element_type=F32) for r0 in (0, tm // 2)], axis=0)
            up = pltpu.roll(gp, 1, 0)
            dn = pltpu.roll(gp, n - 1, 0)
            gate = (up * cwb_ref[0:1, sl] + gp * cwb_ref[1:2, sl] + dn * cwb_ref[2:3, sl])[HALO:HALO + tm] \
                + cwb_ref[3:4, sl]
            act_out[:, sl] = (_gelu_tanh(gate) * vl).astype(BF16)

    def down(act_in):
        return jnp.concatenate([jnp.dot(act_in[r0:r0 + tm // 2, :], wd_ref[...], preferred_element_type=F32)
                                for r0 in (0, tm // 2)], axis=0)

    @pl.when(f == 0)
    def _():
        up_stage(act_a)

    @pl.when(jnp.logical_and(jnp.logical_and(f > 0, f < nf), f % 2 == 1))
    def _():
        up_stage(act_b)
        o_ref[...] += down(act_a)

    @pl.when(jnp.logical_and(jnp.logical_and(f > 0, f < nf), f % 2 == 0))
    def _():
        up_stage(act_a)
        o_ref[...] += down(act_b)

    @pl.when(f == nf)
    def _():
        last = act_a if (D_FF // tf - 1) % 2 == 0 else act_b
        o_ref[...] = _ln_rows(ALPHA * h_ref[...] + (o_ref[...] + down(last)), ln_ref[0:1, :], ln_ref[1:2, :])


def _ffn(h, h_meta, w_in, cwb, wd, ln2, t, tm, tf):
    m = h.shape[0]
    nf = D_FF // tf
    tiles_per_seq = t // tm
    hb = tm // HALO
    last_hb = m // HALO - 1
    up_blk = lambda f: jnp.minimum(f, nf - 1)
    return pl.pallas_call(
        functools.partial(_ffn_kernel, tm=tm, tf=tf, tiles_per_seq=tiles_per_seq),
        out_shape=jax.ShapeDtypeStruct((m, D_MODEL), F32),
        grid=(m // tm, nf + 1),
        in_specs=[pl.BlockSpec((tm, D_MODEL), lambda i, f: (i, 0)),
                  pl.BlockSpec((HALO, D_MODEL), lambda i, f: (jnp.minimum((i + 1) * hb, last_hb), 0)),
                  pl.BlockSpec((N_META, D_MODEL), lambda i, f: (i // tiles_per_seq, 0)),
                  pl.BlockSpec((D_MODEL, tf), lambda i, f: (0, up_blk(f))),
                  pl.BlockSpec((D_MODEL, tf), lambda i, f: (0, up_blk(f) + nf)),
                  pl.BlockSpec((4, tf), lambda i, f: (0, up_blk(f))),
                  pl.BlockSpec((tf, D_MODEL), lambda i, f: (jnp.maximum(f - 1, 0), 0)),
                  pl.BlockSpec((2, D_MODEL), lambda i, f: (0, 0))],
        out_specs=pl.BlockSpec((tm, D_MODEL), lambda i, f: (i, 0)),
        scratch_shapes=[pltpu.VMEM((tm + 2 * HALO, D_MODEL), BF16),
                        pltpu.VMEM((tm, tf), BF16), pltpu.VMEM((tm, tf), BF16),
                        pltpu.VMEM((tf // FFN_SUB, tm + 2 * HALO, FFN_SUB), F32)],
        compiler_params=pltpu.CompilerParams(
            dimension_semantics=("arbitrary", "arbitrary"), vmem_limit_bytes=VMEM_LIMIT),
        name="ffn",
    )(h, h, h_meta, w_in, w_in, cwb, wd, ln2)


def _layer_weights(w_in, w_proj_na, w_proj_sw, w_out, w_ffn_in, w_ffn_down):
    return (w_in[:, QKV_COLS:].astype(BF16), w_in[:, :QKV_COLS].astype(BF16),
            w_proj_na.astype(BF16), w_proj_sw.astype(BF16), w_out.astype(BF16),
            w_ffn_in.astype(BF16), w_ffn_down.astype(BF16))


def _trunk(x, meta_tokens, pm, gm, lg, lb, wts, tables, g1, b1, cwb, ln2):
    b_sz, t, _ = x.shape
    wgate, wqkv, wna, wsw, wout, wffn, wdown = wts
    na_bias, sw_bias, swm_bias = tables
    x2 = x.reshape(b_sz * t, D_MODEL)
    xn, gates = _ln_gates(x2, lg, lb, wgate, tm=512, parts=2)
    p = _qkv_proj(xn, wqkv, tm=1024)
    oa = _na_attn(p, pm, na_bias, b_sz, t)
    ob = _swa_attn(p, pm, sw_bias, b_sz, t, tq=2048)
    oa_m, ob_m = _meta_attn(p, pm, swm_bias, b_sz, t)
    h = _post_attn(x2, lg, lb, oa, ob, gates, wna, wsw, wout, g1, b1, tm=512, parts=2)
    x_m = jnp.tile(meta_tokens, (b_sz, 1))
    gates_m = jnp.tile(gm, (1, b_sz, 1))
    h_m = _post_attn(x_m, lg, lb, oa_m, ob_m, gates_m, wna, wsw, wout, g1, b1, tm=b_sz * N_META, parts=1)
    y = _ffn(h, h_m, wffn, cwb, wdown, ln2, t, tm=512, tf=512)
    return y.reshape(b_sz, t, D_MODEL)


def kernel(x_prompt, x_sample, meta_tokens, ln_emb_g, ln_emb_b, w_in, na_rpb, sw_sink, w_proj_na, w_proj_sw, w_out, ln1_g, ln1_b, w_ffn_in, ffn_conv_w, ffn_conv_b, w_ffn_down, ln2_g, ln2_b):
    assert DEPTH == 1 and w_in.shape[0] == DEPTH
    row = lambda v: v.reshape(1, -1)
    lg, lb = row(ln_emb_g), row(ln_emb_b)
    wts = _layer_weights(w_in[0], w_proj_na[0], w_proj_sw[0], w_out[0], w_ffn_in[0], w_ffn_down[0])
    tables = (_na_bias(na_rpb[0]), _sw_bias(sw_sink[0]), _sw_meta_bias(sw_sink[0]))
    xn_m, gm = _ln_gates(meta_tokens, lg, lb, wts[0], tm=N_META, parts=1)
    pm = _qkv_proj(xn_m, wts[1], tm=N_META)
    cwb = jnp.concatenate([ffn_conv_w[0], row(ffn_conv_b[0])], axis=0)
    ln2 = jnp.stack([ln2_g[0], ln2_b[0]])
    args = (meta_tokens, pm, gm, lg, lb, wts, tables, row(ln1_g[0]), row(ln1_b[0]), cwb, ln2)
    return (_trunk(x_prompt, *args), _trunk(x_sample, *args))
```

```python
import functools

import numpy as np
import jax
import jax.numpy as jnp
from jax import lax
from jax.experimental import pallas as pl
from jax.experimental.pallas import tpu as pltpu

D_MODEL = 2048
N_META = 16
GRID_W = 64
NA_HEADS = 8
HEAD_DIM = 128
NA_WIN_ROWS = 8
NA_WIN_COLS = 16
SW_Q_HEADS = 8
SW_KV_HEADS = 2
SW_GROUP = SW_Q_HEADS // SW_KV_HEADS
SW_WINDOW = 128
SW_BLOCK = 128
D_FF = 5632
LN_EPS = 1e-5
NEG_INF = -1e30
DEPTH = 1
ALPHA = (2 * DEPTH) ** 0.25
SCALE = HEAD_DIM ** -0.5

LANE = 128
BF16_ROWS = 16
MXU_COLS = 256
VMEM_LIMIT = 56 * 1024 * 1024

SLAB_QA, SLAB_KA, SLAB_VA, SLAB_QB, SLAB_KB, SLAB_VB = 0, 8, 16, 24, 32, 34
QKV_COLS = 4608
QKV_SLABS = QKV_COLS // LANE
GATE_COLS = 2 * D_MODEL
GATE_SLABS = GATE_COLS // LANE
PROJ_CHUNK = 2 * MXU_COLS

NA_QROWS = 4
NA_WROWS = 11
NA_QBLK = NA_QROWS * GRID_W
NA_WKEYS = NA_WROWS * GRID_W
NA_KEYS = 768
SW_WKEYS = 3 * SW_BLOCK
SW_KEYS = 512
SW_MKEYS = 256
ATTN_UNROLL = 4

F32 = jnp.float32
BF16 = jnp.bfloat16


def _ln_rows(x, g, b):
    mu = jnp.mean(x, axis=-1, keepdims=True)
    xc = x - mu
    var = jnp.mean(xc * xc, axis=-1, keepdims=True)
    return xc * lax.rsqrt(var + LN_EPS) * g + b


def _scores(q, kcat, bias):
    s = lax.dot_general(q, kcat, (((1,), (1,)), ((), ())), preferred_element_type=F32)
    s = s * SCALE
    return s if bias is None else s + bias


def _softmax_pv_scores(s, vcat):
    m = jnp.max(s, axis=-1, keepdims=True)
    e = jnp.exp(s - m)
    l = jnp.sum(e, axis=-1, keepdims=True)
    o = jnp.dot(e.astype(BF16), vcat, preferred_element_type=F32)
    return o / l


def _softmax_pv(q, kcat, vcat, bias):
    return _softmax_pv_scores(_scores(q, kcat, bias), vcat)


def _pipelined_blocks(n_blocks, scores, finish):
    scores(jnp.int32(0), 0)

    def body(it, carry):
        for u in range(ATTN_UNROLL):
            i = it * ATTN_UNROLL + u
            scores(jnp.minimum(i + 1, n_blocks - 1), (u + 1) % ATTN_UNROLL)
            finish(i, u)
        return carry

    lax.fori_loop(0, n_blocks // ATTN_UNROLL, body, 0)


def _project_to_slabs(xn, w_ref, o_ref, rows):
    for c0 in range(0, w_ref.shape[1], PROJ_CHUNK):
        r = jnp.dot(xn, w_ref[:, c0:c0 + PROJ_CHUNK], preferred_element_type=F32)
        for s in range(PROJ_CHUNK // LANE):
            o_ref[c0 // LANE + s, rows, :] = r[:, s * LANE:(s + 1) * LANE].astype(BF16)


def _ln_gates_kernel(x_ref, g_ref, b_ref, w_ref, xn_ref, o_ref, *, parts):
    hr = x_ref.shape[0] // parts
    for r in range(parts):
        rows = slice(r * hr, (r + 1) * hr)
        xn = _ln_rows(x_ref[rows, :], g_ref[...], b_ref[...]).astype(BF16)
        xn_ref[rows, :] = xn
        _project_to_slabs(xn, w_ref, o_ref, rows)


def _ln_gates(x2, g, b, w, tm, parts):
    m = x2.shape[0]
    const = lambda shape: pl.BlockSpec(shape, lambda i: (0,) * len(shape), pipeline_mode=pl.Buffered(1))
    return pl.pallas_call(
        functools.partial(_ln_gates_kernel, parts=parts),
        out_shape=(jax.ShapeDtypeStruct((m, D_MODEL), BF16),
                   jax.ShapeDtypeStruct((GATE_SLABS, m, LANE), BF16)),
        grid=(m // tm,),
        in_specs=[pl.BlockSpec((tm, D_MODEL), lambda i: (i, 0)),
                  const((1, D_MODEL)), const((1, D_MODEL)), const(w.shape)],
        out_specs=(pl.BlockSpec((tm, D_MODEL), lambda i: (i, 0)),
                   pl.BlockSpec((GATE_SLABS, tm, LANE), lambda i: (0, i, 0))),
        compiler_params=pltpu.CompilerParams(dimension_semantics=("arbitrary",), vmem_limit_bytes=VMEM_LIMIT),
        name="ln_gates",
    )(x2, g, b, w)


def _qkv_kernel(xn_ref, w_ref, o_ref):
    _project_to_slabs(xn_ref[...], w_ref, o_ref, slice(None))


def _qkv_proj(xn, w, tm):
    m = xn.shape[0]
    return pl.pallas_call(
        _qkv_kernel,
        out_shape=jax.ShapeDtypeStruct((QKV_SLABS, m, LANE), BF16),
        grid=(m // tm,),
        in_specs=[pl.BlockSpec((tm, D_MODEL), lambda i: (i, 0)),
                  pl.BlockSpec(w.shape, lambda i: (0, 0), pipeline_mode=pl.Buffered(1))],
        out_specs=pl.BlockSpec((QKV_SLABS, tm, LANE), lambda i: (0, i, 0)),
        compiler_params=pltpu.CompilerParams(dimension_semantics=("arbitrary",), vmem_limit_bytes=VMEM_LIMIT),
        name="qkv_proj",
    )(xn, w)


def _na_kernel(q_ref, k_ref, v_ref, km_ref, vm_ref, bias_ref, o_ref, kcat_ref, vcat_ref, s_ref, *, rows, nblk):
    pad = jnp.zeros((NA_KEYS - NA_WKEYS - N_META, HEAD_DIM), BF16)
    for u in range(ATTN_UNROLL):
        kcat_ref[u, NA_WKEYS:NA_WKEYS + N_META, :] = km_ref[0]
        vcat_ref[u, NA_WKEYS:NA_WKEYS + N_META, :] = vm_ref[0]
        kcat_ref[u, NA_WKEYS + N_META:, :] = pad
        vcat_ref[u, NA_WKEYS + N_META:, :] = pad

    def window_start(i):
        r0 = i * NA_QROWS
        w0 = jnp.minimum(jnp.clip(r0 - NA_WIN_ROWS // 2, 0, rows - NA_WIN_ROWS), rows - NA_WROWS)
        return pl.multiple_of(w0 * GRID_W, GRID_W)

    def scores(i, slot):
        kcat_ref[slot, 0:NA_WKEYS, :] = k_ref[0, pl.ds(window_start(i), NA_WKEYS), :]
        variant = jnp.where(i == 0, 0, jnp.where(i == nblk - 1, 2, 1))
        q = q_ref[0, pl.ds(pl.multiple_of(i * NA_QBLK, NA_QBLK), NA_QBLK), :]
        s_ref[slot] = _scores(q, kcat_ref[slot], bias_ref[variant, 0])

    def finish(i, slot):
        vcat_ref[slot, 0:NA_WKEYS, :] = v_ref[0, pl.ds(window_start(i), NA_WKEYS), :]
        o = _softmax_pv_scores(s_ref[slot], vcat_ref[slot])
        o_ref[0, pl.ds(pl.multiple_of(i * NA_QBLK, NA_QBLK), NA_QBLK), :] = o.astype(BF16)

    _pipelined_blocks(nblk, scores, finish)


def _na_attn(p, pm, bias, b_sz, t):
    rows = t // GRID_W
    nblk = rows // NA_QROWS
    assert rows % NA_QROWS == 0 and nblk >= 3 and rows >= NA_WROWS and nblk % ATTN_UNROLL == 0
    seq = lambda slab: pl.BlockSpec((1, t, HEAD_DIM), lambda b, h: (slab + h, b, 0))
    meta = lambda slab: pl.BlockSpec((1, N_META, HEAD_DIM), lambda b, h: (slab + h, 0, 0))
    return pl.pallas_call(
        functools.partial(_na_kernel, rows=rows, nblk=nblk),
        out_shape=jax.ShapeDtypeStruct((NA_HEADS, b_sz * t, HEAD_DIM), BF16),
        grid=(b_sz, NA_HEADS),
        in_specs=[seq(SLAB_QA), seq(SLAB_KA), seq(SLAB_VA), meta(SLAB_KA), meta(SLAB_VA),
                  pl.BlockSpec((3, 1, NA_QBLK, NA_KEYS), lambda b, h: (0, h, 0, 0))],
        out_specs=pl.BlockSpec((1, t, HEAD_DIM), lambda b, h: (h, b, 0)),
        scratch_shapes=[pltpu.VMEM((ATTN_UNROLL, NA_KEYS, HEAD_DIM), BF16)] * 2
        + [pltpu.VMEM((ATTN_UNROLL, NA_QBLK, NA_KEYS), F32)],
        compiler_params=pltpu.CompilerParams(
            dimension_semantics=("arbitrary", "arbitrary"), vmem_limit_bytes=VMEM_LIMIT),
        name="na_attn",
    )(p, p, p, pm, pm, bias)


def _na_bias(rpb, dtype=F32):
    edge = GRID_W - NA_WIN_COLS
    ext = jnp.pad(rpb.astype(dtype), ((0, 0), (0, 0), (edge, edge)), mode="edge")
    cols = jnp.stack([ext[:, :, GRID_W - 1 - qc:2 * GRID_W - 1 - qc] for qc in range(GRID_W)], axis=2)
    rpad = NA_WROWS - NA_WIN_ROWS
    cols = jnp.pad(cols, ((0, 0), (rpad, rpad), (0, 0), (0, 0)))
    qc = np.arange(GRID_W)[:, None, None]
    j = np.arange(NA_WROWS)[None, :, None]
    kc = np.arange(GRID_W)[None, None, :]
    col_start = np.clip(qc - NA_WIN_COLS // 2, 0, GRID_W - NA_WIN_COLS)
    in_cols = (kc >= col_start) & (kc < col_start + NA_WIN_COLS)
    tables = []
    variants = ([(i, 0) for i in range(NA_QROWS)],
                [(i + NA_WIN_ROWS // 2, i) for i in range(NA_QROWS)],
                [(i + NA_WROWS - NA_QROWS, NA_WROWS - NA_WIN_ROWS) for i in range(NA_QROWS)])
    for variant in variants:
        per_row = []
        for qoff, rs in variant:
            r_lo = (NA_WIN_ROWS - 1) - qoff + rpad
            blk = jnp.transpose(cols[:, r_lo:r_lo + NA_WROWS], (0, 2, 1, 3))
            mask = ((j >= rs) & (j < rs + NA_WIN_ROWS)) & in_cols
            per_row.append(jnp.where(mask[None], blk, NEG_INF))
        tables.append(jnp.stack(per_row, axis=1).reshape(NA_HEADS, NA_QBLK, NA_WKEYS))
    tab = jnp.stack(tables)
    lead = tab.shape[:-1]
    return jnp.concatenate([tab, jnp.zeros(lead + (N_META,), dtype),
                            jnp.full(lead + (NA_KEYS - NA_WKEYS - N_META,), NEG_INF, dtype)], axis=-1)


def _swa_kernel(q_ref, k_ref, v_ref, km_ref, vm_ref, bias_ref, o_ref, kcat_ref, vcat_ref, s_ref, *, t, nb, bps):
    pad = jnp.zeros((SW_KEYS - SW_WKEYS - N_META, HEAD_DIM), BF16)
    for u in range(ATTN_UNROLL):
        kcat_ref[u, SW_WKEYS:SW_WKEYS + N_META, :] = km_ref[0]
        vcat_ref[u, SW_WKEYS:SW_WKEYS + N_META, :] = vm_ref[0]
        kcat_ref[u, SW_WKEYS + N_META:, :] = pad
        vcat_ref[u, SW_WKEYS + N_META:, :] = pad
    step = pl.program_id(2)

    def window_start(i):
        n = step * bps + i
        return pl.multiple_of(jnp.clip((n - 1) * SW_BLOCK, 0, t - SW_WKEYS), SW_BLOCK)

    def scores(i, slot):
        n = step * bps + i
        kcat_ref[slot, 0:SW_WKEYS, :] = k_ref[0, pl.ds(window_start(i), SW_WKEYS), :]
        variant = jnp.where(n == 0, 0, jnp.where(n == nb - 1, 2, 1))
        q = q_ref[:, pl.ds(pl.multiple_of(i * SW_BLOCK, SW_BLOCK), SW_BLOCK), :]
        bias = bias_ref[variant].reshape(SW_GROUP * SW_BLOCK, SW_KEYS)
        s_ref[slot] = _scores(q.reshape(SW_GROUP * SW_BLOCK, HEAD_DIM), kcat_ref[slot], bias)

    def finish(i, slot):
        vcat_ref[slot, 0:SW_WKEYS, :] = v_ref[0, pl.ds(window_start(i), SW_WKEYS), :]
        o = _softmax_pv_scores(s_ref[slot], vcat_ref[slot])
        qs = pl.multiple_of(i * SW_BLOCK, SW_BLOCK)
        o_ref[:, pl.ds(qs, SW_BLOCK), :] = o.reshape(SW_GROUP, SW_BLOCK, HEAD_DIM).astype(BF16)

    _pipelined_blocks(bps, scores, finish)


def _swa_attn(p, pm, bias, b_sz, t, tq):
    nb = t // SW_BLOCK
    assert nb >= 3 and t % tq == 0 and (tq // SW_BLOCK) % ATTN_UNROLL == 0
    steps = t // tq
    bps = tq // SW_BLOCK
    kv = lambda slab: pl.BlockSpec((1, t, HEAD_DIM), lambda b, g, s: (slab + g, b, 0))
    meta = lambda slab: pl.BlockSpec((1, N_META, HEAD_DIM), lambda b, g, s: (slab + g, 0, 0))
    return pl.pallas_call(
        functools.partial(_swa_kernel, t=t, nb=nb, bps=bps),
        out_shape=jax.ShapeDtypeStruct((SW_Q_HEADS, b_sz * t, HEAD_DIM), BF16),
        grid=(b_sz, SW_KV_HEADS, steps),
        in_specs=[pl.BlockSpec((SW_GROUP, tq, HEAD_DIM), lambda b, g, s: (SLAB_QB // SW_GROUP + g, b * steps + s, 0)),
                  kv(SLAB_KB), kv(SLAB_VB), meta(SLAB_KB), meta(SLAB_VB),
                  pl.BlockSpec((3, SW_GROUP, SW_BLOCK, SW_KEYS), lambda b, g, s: (0, g, 0, 0))],
        out_specs=pl.BlockSpec((SW_GROUP, tq, HEAD_DIM), lambda b, g, s: (g, b * steps + s, 0)),
        scratch_shapes=[pltpu.VMEM((ATTN_UNROLL, SW_KEYS, HEAD_DIM), BF16)] * 2
        + [pltpu.VMEM((ATTN_UNROLL, SW_GROUP * SW_BLOCK, SW_KEYS), F32)],
        compiler_params=pltpu.CompilerParams(
            dimension_semantics=("arbitrary", "arbitrary", "arbitrary"), vmem_limit_bytes=VMEM_LIMIT),
        name="swa_attn",
    )(p, p, p, pm, pm, bias)


def _sw_slopes():
    return np.power(2.0, -8.0 * np.arange(1, SW_Q_HEADS + 1, dtype=np.float64) / SW_Q_HEADS).astype(np.float32)


def _sw_bias(sink, dtype=F32):
    i = np.arange(SW_BLOCK)[:, None]
    j = np.arange(SW_WKEYS)[None, :]
    slopes = jnp.asarray(_sw_slopes(), dtype)[:, None, None]
    tables = []
    for qo in (0, SW_BLOCK, 2 * SW_BLOCK):
        dist = np.abs(qo + i - j)
        pen = -(jnp.asarray(dist, dtype)[None] * slopes)
        tables.append(jnp.where((dist <= SW_WINDOW)[None], pen, NEG_INF))
    tab = jnp.stack(tables)
    lead = tab.shape[:-1]
    snk = jnp.broadcast_to(sink.astype(dtype)[None, :, None, None], lead + (1,))
    return jnp.concatenate([tab, jnp.zeros(lead + (N_META,), dtype), snk,
                            jnp.full(lead + (SW_KEYS - SW_WKEYS - N_META - 1,), NEG_INF, dtype)], axis=-1)


def _sw_meta_bias(sink, dtype=F32):
    q = np.arange(N_META)[:, None]
    k = np.arange(SW_BLOCK)[None, :]
    dist = (N_META + k) - q
    slopes = jnp.asarray(_sw_slopes(), dtype)[:, None, None]
    pen = jnp.where((dist <= SW_WINDOW)[None], -(slopes * jnp.asarray(dist, dtype)[None]), NEG_INF)
    lead = pen.shape[:-1]
    snk = jnp.broadcast_to(sink.astype(dtype)[:, None, None], lead + (1,))
    return jnp.concatenate([jnp.zeros(lead + (N_META,), dtype), pen, snk,
                            jnp.full(lead + (SW_MKEYS - N_META - SW_BLOCK - 1,), NEG_INF, dtype)], axis=-1)


def _meta_attn_kernel(pm_ref, k0_ref, v0_ref, bias_ref, oa_ref, ob_ref, kcat_ref, vcat_ref):
    for h in range(NA_HEADS):
        o = _softmax_pv(pm_ref[SLAB_QA + h], pm_ref[SLAB_KA + h], pm_ref[SLAB_VA + h], None)
        oa_ref[h] = o.astype(BF16)
    pad = jnp.zeros((SW_MKEYS - N_META - SW_BLOCK, HEAD_DIM), BF16)
    for g in range(SW_KV_HEADS):
        kcat_ref[0:N_META, :] = pm_ref[SLAB_KB + g]
        vcat_ref[0:N_META, :] = pm_ref[SLAB_VB + g]
        kcat_ref[N_META:N_META + SW_BLOCK, :] = k0_ref[g]
        vcat_ref[N_META:N_META + SW_BLOCK, :] = v0_ref[g]
        kcat_ref[N_META + SW_BLOCK:, :] = pad
        vcat_ref[N_META + SW_BLOCK:, :] = pad
        q = jnp.concatenate([pm_ref[SLAB_QB + g * SW_GROUP + r] for r in range(SW_GROUP)], axis=0)
        bias = bias_ref[g * SW_GROUP:(g + 1) * SW_GROUP].reshape(SW_GROUP * N_META, SW_MKEYS)
        o = _softmax_pv(q, kcat_ref[...], vcat_ref[...], bias)
        for r in range(SW_GROUP):
            ob_ref[g * SW_GROUP + r] = o[r * N_META:(r + 1) * N_META].astype(BF16)


def _meta_attn(p, pm, bias, b_sz, t):
    blocks_per_seq = t // SW_BLOCK
    first_block = lambda slab: pl.BlockSpec((SW_KV_HEADS, SW_BLOCK, HEAD_DIM),
                                            lambda b: (slab // SW_KV_HEADS, b * blocks_per_seq, 0))
    out = jax.ShapeDtypeStruct((NA_HEADS, b_sz * N_META, HEAD_DIM), BF16)
    return pl.pallas_call(
        _meta_attn_kernel,
        out_shape=(out, out),
        grid=(b_sz,),
        in_specs=[pl.BlockSpec((QKV_SLABS, N_META, HEAD_DIM), lambda b: (0, 0, 0)),
                  first_block(SLAB_KB), first_block(SLAB_VB),
                  pl.BlockSpec((SW_Q_HEADS, N_META, SW_MKEYS), lambda b: (0, 0, 0))],
        out_specs=(pl.BlockSpec((NA_HEADS, N_META, HEAD_DIM), lambda b: (0, b, 0)),
                   pl.BlockSpec((SW_Q_HEADS, N_META, HEAD_DIM), lambda b: (0, b, 0))),
        scratch_shapes=[pltpu.VMEM((SW_MKEYS, HEAD_DIM), BF16), pltpu.VMEM((SW_MKEYS, HEAD_DIM), BF16)],
        compiler_params=pltpu.CompilerParams(dimension_semantics=("arbitrary",)),
        name="meta_attn",
    )(pm, p, p, bias)


def _post_kernel(x_ref, lg_ref, lb_ref, oa_ref, ob_ref, ga_ref, gb_ref, wna_ref, wsw_ref, wout_ref,
                 g1_ref, b1_ref, h_ref, *, parts):
    hr = x_ref.shape[0] // parts
    for r in range(parts):
        rows = slice(r * hr, (r + 1) * hr)
        oa = jnp.concatenate([oa_ref[h, rows, :] for h in range(NA_HEADS)], axis=1)
        ob = jnp.concatenate([ob_ref[h, rows, :] for h in range(SW_Q_HEADS)], axis=1)
        merged = []
        for c0 in range(0, D_MODEL, PROJ_CHUNK):
            cols = slice(c0, c0 + PROJ_CHUNK)
            slabs = range(c0 // LANE, (c0 + PROJ_CHUNK) // LANE)
            a = jnp.dot(oa, wna_ref[:, cols], preferred_element_type=F32)
            b = jnp.dot(ob, wsw_ref[:, cols], preferred_element_type=F32)
            ga = jnp.concatenate([ga_ref[s, rows, :] for s in slabs], axis=1).astype(F32)
            gb = jnp.concatenate([gb_ref[s, rows, :] for s in slabs], axis=1).astype(F32)
            merged.append((jax.nn.sigmoid(ga) * a + jax.nn.sigmoid(gb) * b).astype(BF16))
        merged = jnp.concatenate(merged, axis=1)
        y = jnp.dot(merged, wout_ref[...], preferred_element_type=F32)
        h0 = _ln_rows(x_ref[rows, :], lg_ref[...], lb_ref[...])
        h_ref[rows, :] = _ln_rows(ALPHA * h0 + y, g1_ref[...], b1_ref[...])


def _post_attn(x2, lg, lb, oa, ob, gates, wna, wsw, wout, g1, b1, tm, parts):
    m = x2.shape[0]
    n_g = D_MODEL // LANE
    const = lambda shape: pl.BlockSpec(shape, lambda i: (0,) * len(shape), pipeline_mode=pl.Buffered(1))
    heads = pl.BlockSpec((NA_HEADS, tm, HEAD_DIM), lambda i: (0, i, 0))
    return pl.pallas_call(
        functools.partial(_post_kernel, parts=parts),
        out_shape=jax.ShapeDtypeStruct((m, D_MODEL), F32),
        grid=(m // tm,),
        in_specs=[pl.BlockSpec((tm, D_MODEL), lambda i: (i, 0)),
                  const((1, D_MODEL)), const((1, D_MODEL)),
                  heads, heads,
                  pl.BlockSpec((n_g, tm, LANE), lambda i: (0, i, 0)),
                  pl.BlockSpec((n_g, tm, LANE), lambda i: (1, i, 0)),
                  const(wna.shape), const(wsw.shape), const(wout.shape),
                  const((1, D_MODEL)), const((1, D_MODEL))],
        out_specs=pl.BlockSpec((tm, D_MODEL), lambda i: (i, 0)),
        compiler_params=pltpu.CompilerParams(dimension_semantics=("arbitrary",), vmem_limit_bytes=VMEM_LIMIT),
        name="post_attn",
    )(x2, lg, lb, oa, ob, gates, gates, wna, wsw, wout, g1, b1)


HALO = BF16_ROWS
FFN_SUB = MXU_COLS


_GELU_C = 2.0 * float(np.sqrt(2.0 / np.pi)) * float(np.log2(np.e))


def _gelu_tanh(x):
    u = x * (x * x * (-_GELU_C * 0.044715) - _GELU_C)
    return x / (1.0 + jnp.exp2(u))


def _ffn_kernel(h_ref, hn_ref, hm_ref, wg_ref, wv_ref, cwb_ref, wd_ref, ln_ref, o_ref, hb_ref, act_a, act_b,
                *, tm, tf, tiles_per_seq):
    i = pl.program_id(0)
    f = pl.program_id(1)
    nf = pl.num_programs(1) - 1
    pos = i % tiles_per_seq

    @pl.when(jnp.logical_and(f == 0, pos == 0))
    def _():
        hb_ref[0:HALO, :] = hm_ref[...].astype(BF16)

    @pl.when(jnp.logical_and(f == 0, pos != 0))
    def _():
        hb_ref[0:HALO, :] = hb_ref[tm:tm + HALO, :]

    @pl.when(f == 0)
    def _():
        nxt = jnp.where(pos == tiles_per_seq - 1, 0.0, hn_ref[...])
        hb_ref[HALO:HALO + tm, :] = h_ref[...].astype(BF16)
        hb_ref[HALO + tm:, :] = nxt.astype(BF16)
        o_ref[...] = jnp.zeros_like(o_ref)

    n = tm + 2 * HALO

    def up_stage(act_out):
        for c0 in range(0, tf, FFN_SUB):
            sl = slice(c0, c0 + FFN_SUB)
            gp = jnp.concatenate([jnp.dot(hb_ref[r0:r0 + n // 2, :], wg_ref[:, sl], preferred_element_type=F32)
                                  for r0 in (0, n // 2)], axis=0)
            vl = jnp.concatenate([jnp.dot(hb_ref[HALO + r0:HALO + r0 + tm // 2, :], wv_ref[:, sl],
                                          preferred_element_type=F32) for r0 in (0, tm // 2)], axis=0)
            up = pltpu.roll(gp, 1, 0)
            dn = pltpu.roll(gp, n - 1, 0)
            gate = (up * cwb_ref[0:1, sl] + gp * cwb_ref[1:2, sl] + dn * cwb_ref[2:3, sl])[HALO:HALO + tm] \
                + cwb_ref[3:4, sl]
            act_out[:, sl] = (_gelu_tanh(gate) * vl).astype(BF16)

    def down(act_in):
        return jnp.concatenate([jnp.dot(act_in[r0:r0 + tm // 2, :], wd_ref[...], preferred_element_type=F32)
                                for r0 in (0, tm // 2)], axis=0)

    @pl.when(f == 0)
    def _():
        up_stage(act_a)

    @pl.when(jnp.logical_and(jnp.logical_and(f > 0, f < nf), f % 2 == 1))
    def _():
        up_stage(act_b)
        o_ref[...] += down(act_a)

    @pl.when(jnp.logical_and(jnp.logical_and(f > 0, f < nf), f % 2 == 0))
    def _():
        up_stage(act_a)
        o_ref[...] += down(act_b)

    @pl.when(f == nf)
    def _():
        last = act_a if (D_FF // tf - 1) % 2 == 0 else act_b
        o_ref[...] = _ln_rows(ALPHA * h_ref[...] + (o_ref[...] + down(last)), ln_ref[0:1, :], ln_ref[1:2, :])


def _ffn(h, h_meta, w_in, cwb, wd, ln2, t, tm, tf):
    m = h.shape[0]
    nf = D_FF // tf
    tiles_per_seq = t // tm
    hb = tm // HALO
    last_hb = m // HALO - 1
    single = pl.Buffered(1)
    up_blk = lambda f: jnp.minimum(f, nf - 1)
    return pl.pallas_call(
        functools.partial(_ffn_kernel, tm=tm, tf=tf, tiles_per_seq=tiles_per_seq),
        out_shape=jax.ShapeDtypeStruct((m, D_MODEL), F32),
        grid=(m // tm, nf + 1),
        in_specs=[pl.BlockSpec((tm, D_MODEL), lambda i, f: (i, 0), pipeline_mode=single),
                  pl.BlockSpec((HALO, D_MODEL), lambda i, f: (jnp.minimum((i + 1) * hb, last_hb), 0)),
                  pl.BlockSpec((N_META, D_MODEL), lambda i, f: (i // tiles_per_seq, 0)),
                  pl.BlockSpec((D_MODEL, tf), lambda i, f: (0, up_blk(f))),
                  pl.BlockSpec((D_MODEL, tf), lambda i, f: (0, up_blk(f) + nf)),
                  pl.BlockSpec((4, tf), lambda i, f: (0, up_blk(f))),
                  pl.BlockSpec((tf, D_MODEL), lambda i, f: (jnp.maximum(f - 1, 0), 0)),
                  pl.BlockSpec((2, D_MODEL), lambda i, f: (0, 0))],
        out_specs=pl.BlockSpec((tm, D_MODEL), lambda i, f: (i, 0), pipeline_mode=single),
        scratch_shapes=[pltpu.VMEM((tm + 2 * HALO, D_MODEL), BF16),
                        pltpu.VMEM((tm, tf), BF16), pltpu.VMEM((tm, tf), BF16)],
        compiler_params=pltpu.CompilerParams(
            dimension_semantics=("arbitrary", "arbitrary"), vmem_limit_bytes=VMEM_LIMIT),
        name="ffn",
    )(h, h, h_meta, w_in, w_in, cwb, wd, ln2)


def _layer_weights(w_in, w_proj_na, w_proj_sw, w_out, w_ffn_in, w_ffn_down):
    return (w_in[:, QKV_COLS:].astype(BF16), w_in[:, :QKV_COLS].astype(BF16),
            w_proj_na.astype(BF16), w_proj_sw.astype(BF16), w_out.astype(BF16),
            w_ffn_in.astype(BF16), w_ffn_down.astype(BF16))


def _trunk(x, meta_tokens, pm, gm, lg, lb, wts, tables, g1, b1, cwb, ln2):
    b_sz, t, _ = x.shape
    wgate, wqkv, wna, wsw, wout, wffn, wdown = wts
    na_bias, sw_bias, swm_bias = tables
    x2 = x.reshape(b_sz * t, D_MODEL)
    xn, gates = _ln_gates(x2, lg, lb, wgate, tm=512, parts=2)
    p = _qkv_proj(xn, wqkv, tm=1024)
    oa = _na_attn(p, pm, na_bias, b_sz, t)
    ob = _swa_attn(p, pm, sw_bias, b_sz, t, tq=2048)
    oa_m, ob_m = _meta_attn(p, pm, swm_bias, b_sz, t)
    h = _post_attn(x2, lg, lb, oa, ob, gates, wna, wsw, wout, g1, b1, tm=512, parts=2)
    x_m = jnp.tile(meta_tokens, (b_sz, 1))
    gates_m = jnp.tile(gm, (1, b_sz, 1))
    h_m = _post_attn(x_m, lg, lb, oa_m, ob_m, gates_m, wna, wsw, wout, g1, b1, tm=b_sz * N_META, parts=1)
    y = _ffn(h, h_m, wffn, cwb, wdown, ln2, t, tm=1024, tf=512)
    return y.reshape(b_sz, t, D_MODEL)


def kernel(x_prompt, x_sample, meta_tokens, ln_emb_g, ln_emb_b, w_in, na_rpb, sw_sink, w_proj_na, w_proj_sw, w_out, ln1_g, ln1_b, w_ffn_in, ffn_conv_w, ffn_conv_b, w_ffn_down, ln2_g, ln2_b):
    assert DEPTH == 1 and w_in.shape[0] == DEPTH
    row = lambda v: v.reshape(1, -1)
    lg, lb = row(ln_emb_g), row(ln_emb_b)
    wts = _layer_weights(w_in[0], w_proj_na[0], w_proj_sw[0], w_out[0], w_ffn_in[0], w_ffn_down[0])
    tables = (_na_bias(na_rpb[0]), _sw_bias(sw_sink[0]), _sw_meta_bias(sw_sink[0]))
    xn_m, gm = _ln_gates(meta_tokens, lg, lb, wts[0], tm=N_META, parts=1)
    pm = _qkv_proj(xn_m, wts[1], tm=N_META)
    cwb = jnp.concatenate([ffn_conv_w[0], row(ffn_conv_b[0])], axis=0)
    ln2 = jnp.stack([ln2_g[0], ln2_b[0]])
    args = (meta_tokens, pm, gm, lg, lb, wts, tables, row(ln1_g[0]), row(ln1_b[0]), cwb, ln2)
    return (_trunk(x_prompt, *args), _trunk(x_sample, *args))
```

```python
import functools

import numpy as np
import jax
import jax.numpy as jnp
from jax import lax
from jax.experimental import pallas as pl
from jax.experimental.pallas import tpu as pltpu

D_MODEL = 2048
N_META = 16
GRID_W = 64
NA_HEADS = 8
HEAD_DIM = 128
NA_WIN_ROWS = 8
NA_WIN_COLS = 16
SW_Q_HEADS = 8
SW_KV_HEADS = 2
SW_GROUP = SW_Q_HEADS // SW_KV_HEADS
SW_WINDOW = 128
SW_BLOCK = 128
D_FF = 5632
LN_EPS = 1e-5
NEG_INF = -1e30
DEPTH = 1
ALPHA = (2 * DEPTH) ** 0.25
SCALE = HEAD_DIM ** -0.5

LANE = 128
BF16_ROWS = 16
MXU_COLS = 256
VMEM_LIMIT = 56 * 1024 * 1024

SLAB_QA, SLAB_KA, SLAB_VA, SLAB_QB, SLAB_KB, SLAB_VB = 0, 8, 16, 24, 32, 34
QKV_COLS = 4608
QKV_SLABS = QKV_COLS // LANE
GATE_COLS = 2 * D_MODEL
GATE_SLABS = GATE_COLS // LANE
PROJ_CHUNK = 2 * MXU_COLS

NA_QROWS = 4
NA_WROWS = 11
NA_QBLK = NA_QROWS * GRID_W
NA_WKEYS = NA_WROWS * GRID_W
NA_KEYS = 768
SW_WKEYS = 3 * SW_BLOCK
SW_KEYS = 512
SW_MKEYS = 256
ATTN_UNROLL = 4

F32 = jnp.float32
BF16 = jnp.bfloat16


def _ln_rows(x, g, b):
    mu = jnp.mean(x, axis=-1, keepdims=True)
    xc = x - mu
    var = jnp.mean(xc * xc, axis=-1, keepdims=True)
    return xc * lax.rsqrt(var + LN_EPS) * g + b


def _scores(q, kcat, bias):
    s = lax.dot_general(q, kcat, (((1,), (1,)), ((), ())), preferred_element_type=F32)
    s = s * SCALE
    return s if bias is None else s + bias


def _softmax_pv_scores(s, vcat):
    m = jnp.max(s, axis=-1, keepdims=True)
    e = jnp.exp(s - m)
    l = jnp.sum(e, axis=-1, keepdims=True)
    o = jnp.dot(e.astype(BF16), vcat, preferred_element_type=F32)
    return o / l


def _softmax_pv(q, kcat, vcat, bias):
    return _softmax_pv_scores(_scores(q, kcat, bias), vcat)


def _pipelined_blocks(n_blocks, scores, finish):
    scores(jnp.int32(0), 0)

    def body(it, carry):
        for u in range(ATTN_UNROLL):
            i = it * ATTN_UNROLL + u
            scores(jnp.minimum(i + 1, n_blocks - 1), (u + 1) % ATTN_UNROLL)
            finish(i, u)
        return carry

    lax.fori_loop(0, n_blocks // ATTN_UNROLL, body, 0)


def _project_to_slabs(xn, w_ref, o_ref, rows):
    for c0 in range(0, w_ref.shape[1], PROJ_CHUNK):
        r = jnp.dot(xn, w_ref[:, c0:c0 + PROJ_CHUNK], preferred_element_type=F32)
        for s in range(PROJ_CHUNK // LANE):
            o_ref[c0 // LANE + s, rows, :] = r[:, s * LANE:(s + 1) * LANE].astype(BF16)


def _ln_gates_kernel(x_ref, g_ref, b_ref, w_ref, xn_ref, o_ref, *, parts):
    hr = x_ref.shape[0] // parts
    for r in range(parts):
        rows = slice(r * hr, (r + 1) * hr)
        xn = _ln_rows(x_ref[rows, :], g_ref[...], b_ref[...]).astype(BF16)
        xn_ref[rows, :] = xn
        _project_to_slabs(xn, w_ref, o_ref, rows)


def _ln_gates(x2, g, b, w, tm, parts):
    m = x2.shape[0]
    const = lambda shape: pl.BlockSpec(shape, lambda i: (0,) * len(shape), pipeline_mode=pl.Buffered(1))
    return pl.pallas_call(
        functools.partial(_ln_gates_kernel, parts=parts),
        out_shape=(jax.ShapeDtypeStruct((m, D_MODEL), BF16),
                   jax.ShapeDtypeStruct((GATE_SLABS, m, LANE), BF16)),
        grid=(m // tm,),
        in_specs=[pl.BlockSpec((tm, D_MODEL), lambda i: (i, 0)),
                  const((1, D_MODEL)), const((1, D_MODEL)), const(w.shape)],
        out_specs=(pl.BlockSpec((tm, D_MODEL), lambda i: (i, 0)),
                   pl.BlockSpec((GATE_SLABS, tm, LANE), lambda i: (0, i, 0))),
        compiler_params=pltpu.CompilerParams(dimension_semantics=("arbitrary",), vmem_limit_bytes=VMEM_LIMIT),
        name="ln_gates",
    )(x2, g, b, w)


def _qkv_kernel(xn_ref, w_ref, o_ref):
    _project_to_slabs(xn_ref[...], w_ref, o_ref, slice(None))


def _qkv_proj(xn, w, tm):
    m = xn.shape[0]
    return pl.pallas_call(
        _qkv_kernel,
        out_shape=jax.ShapeDtypeStruct((QKV_SLABS, m, LANE), BF16),
        grid=(m // tm,),
        in_specs=[pl.BlockSpec((tm, D_MODEL), lambda i: (i, 0)),
                  pl.BlockSpec(w.shape, lambda i: (0, 0), pipeline_mode=pl.Buffered(1))],
        out_specs=pl.BlockSpec((QKV_SLABS, tm, LANE), lambda i: (0, i, 0)),
        compiler_params=pltpu.CompilerParams(dimension_semantics=("arbitrary",), vmem_limit_bytes=VMEM_LIMIT),
        name="qkv_proj",
    )(xn, w)


def _na_kernel(q_ref, k_ref, v_ref, km_ref, vm_ref, bias_ref, o_ref, kcat_ref, vcat_ref, s_ref, *, rows, nblk):
    pad = jnp.zeros((NA_KEYS - NA_WKEYS - N_META, HEAD_DIM), BF16)
    for u in range(ATTN_UNROLL):
        kcat_ref[u, NA_WKEYS:NA_WKEYS + N_META, :] = km_ref[0]
        vcat_ref[u, NA_WKEYS:NA_WKEYS + N_META, :] = vm_ref[0]
        kcat_ref[u, NA_WKEYS + N_META:, :] = pad
        vcat_ref[u, NA_WKEYS + N_META:, :] = pad

    def window_start(i):
        r0 = i * NA_QROWS
        w0 = jnp.minimum(jnp.clip(r0 - NA_WIN_ROWS // 2, 0, rows - NA_WIN_ROWS), rows - NA_WROWS)
        return pl.multiple_of(w0 * GRID_W, GRID_W)

    def scores(i, slot):
        kcat_ref[slot, 0:NA_WKEYS, :] = k_ref[0, pl.ds(window_start(i), NA_WKEYS), :]
        variant = jnp.where(i == 0, 0, jnp.where(i == nblk - 1, 2, 1))
        q = q_ref[0, pl.ds(pl.multiple_of(i * NA_QBLK, NA_QBLK), NA_QBLK), :]
        s_ref[slot] = _scores(q, kcat_ref[slot], bias_ref[variant, 0])

    def finish(i, slot):
        vcat_ref[slot, 0:NA_WKEYS, :] = v_ref[0, pl.ds(window_start(i), NA_WKEYS), :]
        o = _softmax_pv_scores(s_ref[slot], vcat_ref[slot])
        o_ref[0, pl.ds(pl.multiple_of(i * NA_QBLK, NA_QBLK), NA_QBLK), :] = o.astype(BF16)

    _pipelined_blocks(nblk, scores, finish)


def _na_attn(p, pm, bias, b_sz, t):
    rows = t // GRID_W
    nblk = rows // NA_QROWS
    assert rows % NA_QROWS == 0 and nblk >= 3 and rows >= NA_WROWS and nblk % ATTN_UNROLL == 0
    seq = lambda slab: pl.BlockSpec((1, t, HEAD_DIM), lambda b, h: (slab + h, b, 0))
    meta = lambda slab: pl.BlockSpec((1, N_META, HEAD_DIM), lambda b, h: (slab + h, 0, 0))
    return pl.pallas_call(
        functools.partial(_na_kernel, rows=rows, nblk=nblk),
        out_shape=jax.ShapeDtypeStruct((NA_HEADS, b_sz * t, HEAD_DIM), BF16),
        grid=(b_sz, NA_HEADS),
        in_specs=[seq(SLAB_QA), seq(SLAB_KA), seq(SLAB_VA), meta(SLAB_KA), meta(SLAB_VA),
                  pl.BlockSpec((3, 1, NA_QBLK, NA_KEYS), lambda b, h: (0, h, 0, 0))],
        out_specs=pl.BlockSpec((1, t, HEAD_DIM), lambda b, h: (h, b, 0)),
        scratch_shapes=[pltpu.VMEM((ATTN_UNROLL, NA_KEYS, HEAD_DIM), BF16)] * 2
        + [pltpu.VMEM((ATTN_UNROLL, NA_QBLK, NA_KEYS), F32)],
        compiler_params=pltpu.CompilerParams(
            dimension_semantics=("arbitrary", "arbitrary"), vmem_limit_bytes=VMEM_LIMIT),
        name="na_attn",
    )(p, p, p, pm, pm, bias)


def _na_bias(rpb, dtype=F32):
    edge = GRID_W - NA_WIN_COLS
    ext = jnp.pad(rpb.astype(dtype), ((0, 0), (0, 0), (edge, edge + 1)), mode="edge")
    n_r, span = ext.shape[1], 2 * GRID_W
    stream = jnp.broadcast_to(ext[:, :, None, :], (NA_HEADS, n_r, GRID_W, span)).reshape(NA_HEADS, n_r, GRID_W * span)
    skew = stream[:, :, :GRID_W * (span - 1)].reshape(NA_HEADS, n_r, GRID_W, span - 1)
    cols = skew[:, :, :, GRID_W - 1:2 * GRID_W - 1]
    rpad = NA_WROWS - NA_WIN_ROWS
    cols = jnp.pad(cols, ((0, 0), (rpad, rpad), (0, 0), (0, 0)))
    qc = np.arange(GRID_W)[:, None, None]
    j = np.arange(NA_WROWS)[None, :, None]
    kc = np.arange(GRID_W)[None, None, :]
    col_start = np.clip(qc - NA_WIN_COLS // 2, 0, GRID_W - NA_WIN_COLS)
    in_cols = (kc >= col_start) & (kc < col_start + NA_WIN_COLS)
    tables = []
    variants = ([(i, 0) for i in range(NA_QROWS)],
                [(i + NA_WIN_ROWS // 2, i) for i in range(NA_QROWS)],
                [(i + NA_WROWS - NA_QROWS, NA_WROWS - NA_WIN_ROWS) for i in range(NA_QROWS)])
    for variant in variants:
        per_row = []
        for qoff, rs in variant:
            r_lo = (NA_WIN_ROWS - 1) - qoff + rpad
            blk = jnp.transpose(cols[:, r_lo:r_lo + NA_WROWS], (0, 2, 1, 3))
            mask = ((j >= rs) & (j < rs + NA_WIN_ROWS)) & in_cols
            per_row.append(jnp.where(mask[None], blk, NEG_INF))
        tables.append(jnp.stack(per_row, axis=1).reshape(NA_HEADS, NA_QBLK, NA_WKEYS))
    tab = jnp.stack(tables)
    lead = tab.shape[:-1]
    return jnp.concatenate([tab, jnp.zeros(lead + (N_META,), dtype),
                            jnp.full(lead + (NA_KEYS - NA_WKEYS - N_META,), NEG_INF, dtype)], axis=-1)


def _swa_kernel(q_ref, k_ref, v_ref, km_ref, vm_ref, bias_ref, o_ref, kcat_ref, vcat_ref, s_ref, *, t, nb, bps):
    pad = jnp.zeros((SW_KEYS - SW_WKEYS - N_META, HEAD_DIM), BF16)
    for u in range(ATTN_UNROLL):
        kcat_ref[u, SW_WKEYS:SW_WKEYS + N_META, :] = km_ref[0]
        vcat_ref[u, SW_WKEYS:SW_WKEYS + N_META, :] = vm_ref[0]
        kcat_ref[u, SW_WKEYS + N_META:, :] = pad
        vcat_ref[u, SW_WKEYS + N_META:, :] = pad
    step = pl.program_id(2)

    def window_start(i):
        n = step * bps + i
        return pl.multiple_of(jnp.clip((n - 1) * SW_BLOCK, 0, t - SW_WKEYS), SW_BLOCK)

    def scores(i, slot):
        n = step * bps + i
        kcat_ref[slot, 0:SW_WKEYS, :] = k_ref[0, pl.ds(window_start(i), SW_WKEYS), :]
        variant = jnp.where(n == 0, 0, jnp.where(n == nb - 1, 2, 1))
        q = q_ref[:, pl.ds(pl.multiple_of(i * SW_BLOCK, SW_BLOCK), SW_BLOCK), :]
        bias = bias_ref[variant].reshape(SW_GROUP * SW_BLOCK, SW_KEYS)
        s_ref[slot] = _scores(q.reshape(SW_GROUP * SW_BLOCK, HEAD_DIM), kcat_ref[slot], bias)

    def finish(i, slot):
        vcat_ref[slot, 0:SW_WKEYS, :] = v_ref[0, pl.ds(window_start(i), SW_WKEYS), :]
        o = _softmax_pv_scores(s_ref[slot], vcat_ref[slot])
        qs = pl.multiple_of(i * SW_BLOCK, SW_BLOCK)
        o_ref[:, pl.ds(qs, SW_BLOCK), :] = o.reshape(SW_GROUP, SW_BLOCK, HEAD_DIM).astype(BF16)

    _pipelined_blocks(bps, scores, finish)


def _swa_attn(p, pm, bias, b_sz, t, tq):
    nb = t // SW_BLOCK
    assert nb >= 3 and t % tq == 0 and (tq // SW_BLOCK) % ATTN_UNROLL == 0
    steps = t // tq
    bps = tq // SW_BLOCK
    kv = lambda slab: pl.BlockSpec((1, t, HEAD_DIM), lambda b, g, s: (slab + g, b, 0))
    meta = lambda slab: pl.BlockSpec((1, N_META, HEAD_DIM), lambda b, g, s: (slab + g, 0, 0))
    return pl.pallas_call(
        functools.partial(_swa_kernel, t=t, nb=nb, bps=bps),
        out_shape=jax.ShapeDtypeStruct((SW_Q_HEADS, b_sz * t, HEAD_DIM), BF16),
        grid=(b_sz, SW_KV_HEADS, steps),
        in_specs=[pl.BlockSpec((SW_GROUP, tq, HEAD_DIM), lambda b, g, s: (SLAB_QB // SW_GROUP + g, b * steps + s, 0)),
                  kv(SLAB_KB), kv(SLAB_VB), meta(SLAB_KB), meta(SLAB_VB),
                  pl.BlockSpec((3, SW_GROUP, SW_BLOCK, SW_KEYS), lambda b, g, s: (0, g, 0, 0))],
        out_specs=pl.BlockSpec((SW_GROUP, tq, HEAD_DIM), lambda b, g, s: (g, b * steps + s, 0)),
        scratch_shapes=[pltpu.VMEM((ATTN_UNROLL, SW_KEYS, HEAD_DIM), BF16)] * 2
        + [pltpu.VMEM((ATTN_UNROLL, SW_GROUP * SW_BLOCK, SW_KEYS), F32)],
        compiler_params=pltpu.CompilerParams(
            dimension_semantics=("arbitrary", "arbitrary", "arbitrary"), vmem_limit_bytes=VMEM_LIMIT),
        name="swa_attn",
    )(p, p, p, pm, pm, bias)


def _sw_slopes():
    return np.power(2.0, -8.0 * np.arange(1, SW_Q_HEADS + 1, dtype=np.float64) / SW_Q_HEADS).astype(np.float32)


def _sw_bias(sink, dtype=F32):
    i = np.arange(SW_BLOCK)[:, None]
    j = np.arange(SW_WKEYS)[None, :]
    slopes = jnp.asarray(_sw_slopes(), dtype)[:, None, None]
    tables = []
    for qo in (0, SW_BLOCK, 2 * SW_BLOCK):
        dist = np.abs(qo + i - j)
        pen = -(jnp.asarray(dist, dtype)[None] * slopes)
        tables.append(jnp.where((dist <= SW_WINDOW)[None], pen, NEG_INF))
    tab = jnp.stack(tables)
    lead = tab.shape[:-1]
    snk = jnp.broadcast_to(sink.astype(dtype)[None, :, None, None], lead + (1,))
    return jnp.concatenate([tab, jnp.zeros(lead + (N_META,), dtype), snk,
                            jnp.full(lead + (SW_KEYS - SW_WKEYS - N_META - 1,), NEG_INF, dtype)], axis=-1)


def _sw_meta_bias(sink, dtype=F32):
    q = np.arange(N_META)[:, None]
    k = np.arange(SW_BLOCK)[None, :]
    dist = (N_META + k) - q
    slopes = jnp.asarray(_sw_slopes(), dtype)[:, None, None]
    pen = jnp.where((dist <= SW_WINDOW)[None], -(slopes * jnp.asarray(dist, dtype)[None]), NEG_INF)
    lead = pen.shape[:-1]
    snk = jnp.broadcast_to(sink.astype(dtype)[:, None, None], lead + (1,))
    return jnp.concatenate([jnp.zeros(lead + (N_META,), dtype), pen, snk,
                            jnp.full(lead + (SW_MKEYS - N_META - SW_BLOCK - 1,), NEG_INF, dtype)], axis=-1)


def _meta_attn_kernel(pm_ref, k0_ref, v0_ref, bias_ref, oa_ref, ob_ref, kcat_ref, vcat_ref):
    for h in range(NA_HEADS):
        o = _softmax_pv(pm_ref[SLAB_QA + h], pm_ref[SLAB_KA + h], pm_ref[SLAB_VA + h], None)
        oa_ref[h] = o.astype(BF16)
    pad = jnp.zeros((SW_MKEYS - N_META - SW_BLOCK, HEAD_DIM), BF16)
    for g in range(SW_KV_HEADS):
        kcat_ref[0:N_META, :] = pm_ref[SLAB_KB + g]
        vcat_ref[0:N_META, :] = pm_ref[SLAB_VB + g]
        kcat_ref[N_META:N_META + SW_BLOCK, :] = k0_ref[g]
        vcat_ref[N_META:N_META + SW_BLOCK, :] = v0_ref[g]
        kcat_ref[N_META + SW_BLOCK:, :] = pad
        vcat_ref[N_META + SW_BLOCK:, :] = pad
        q = jnp.concatenate([pm_ref[SLAB_QB + g * SW_GROUP + r] for r in range(SW_GROUP)], axis=0)
        bias = bias_ref[g * SW_GROUP:(g + 1) * SW_GROUP].reshape(SW_GROUP * N_META, SW_MKEYS)
        o = _softmax_pv(q, kcat_ref[...], vcat_ref[...], bias)
        for r in range(SW_GROUP):
            ob_ref[g * SW_GROUP + r] = o[r * N_META:(r + 1) * N_META].astype(BF16)


def _meta_attn(p, pm, bias, b_sz, t):
    blocks_per_seq = t // SW_BLOCK
    first_block = lambda slab: pl.BlockSpec((SW_KV_HEADS, SW_BLOCK, HEAD_DIM),
                                            lambda b: (slab // SW_KV_HEADS, b * blocks_per_seq, 0))
    out = jax.ShapeDtypeStruct((NA_HEADS, b_sz * N_META, HEAD_DIM), BF16)
    return pl.pallas_call(
        _meta_attn_kernel,
        out_shape=(out, out),
        grid=(b_sz,),
        in_specs=[pl.BlockSpec((QKV_SLABS, N_META, HEAD_DIM), lambda b: (0, 0, 0)),
                  first_block(SLAB_KB), first_block(SLAB_VB),
                  pl.BlockSpec((SW_Q_HEADS, N_META, SW_MKEYS), lambda b: (0, 0, 0))],
        out_specs=(pl.BlockSpec((NA_HEADS, N_META, HEAD_DIM), lambda b: (0, b, 0)),
                   pl.BlockSpec((SW_Q_HEADS, N_META, HEAD_DIM), lambda b: (0, b, 0))),
        scratch_shapes=[pltpu.VMEM((SW_MKEYS, HEAD_DIM), BF16), pltpu.VMEM((SW_MKEYS, HEAD_DIM), BF16)],
        compiler_params=pltpu.CompilerParams(dimension_semantics=("arbitrary",)),
        name="meta_attn",
    )(pm, p, p, bias)


def _post_kernel(x_ref, lg_ref, lb_ref, oa_ref, ob_ref, ga_ref, gb_ref, wna_ref, wsw_ref, wout_ref,
                 g1_ref, b1_ref, h_ref, *, parts):
    hr = x_ref.shape[0] // parts
    for r in range(parts):
        rows = slice(r * hr, (r + 1) * hr)
        oa = jnp.concatenate([oa_ref[h, rows, :] for h in range(NA_HEADS)], axis=1)
        ob = jnp.concatenate([ob_ref[h, rows, :] for h in range(SW_Q_HEADS)], axis=1)
        merged = []
        for c0 in range(0, D_MODEL, PROJ_CHUNK):
            cols = slice(c0, c0 + PROJ_CHUNK)
            slabs = range(c0 // LANE, (c0 + PROJ_CHUNK) // LANE)
            a = jnp.dot(oa, wna_ref[:, cols], preferred_element_type=F32)
            b = jnp.dot(ob, wsw_ref[:, cols], preferred_element_type=F32)
            ga = jnp.concatenate([ga_ref[s, rows, :] for s in slabs], axis=1).astype(F32)
            gb = jnp.concatenate([gb_ref[s, rows, :] for s in slabs], axis=1).astype(F32)
            merged.append((jax.nn.sigmoid(ga) * a + jax.nn.sigmoid(gb) * b).astype(BF16))
        merged = jnp.concatenate(merged, axis=1)
        y = jnp.dot(merged, wout_ref[...], preferred_element_type=F32)
        h0 = _ln_rows(x_ref[rows, :], lg_ref[...], lb_ref[...])
        h_ref[rows, :] = _ln_rows(ALPHA * h0 + y, g1_ref[...], b1_ref[...])


def _post_attn(x2, lg, lb, oa, ob, gates, wna, wsw, wout, g1, b1, tm, parts):
    m = x2.shape[0]
    n_g = D_MODEL // LANE
    const = lambda shape: pl.BlockSpec(shape, lambda i: (0,) * len(shape), pipeline_mode=pl.Buffered(1))
    heads = pl.BlockSpec((NA_HEADS, tm, HEAD_DIM), lambda i: (0, i, 0))
    return pl.pallas_call(
        functools.partial(_post_kernel, parts=parts),
        out_shape=jax.ShapeDtypeStruct((m, D_MODEL), F32),
        grid=(m // tm,),
        in_specs=[pl.BlockSpec((tm, D_MODEL), lambda i: (i, 0)),
                  const((1, D_MODEL)), const((1, D_MODEL)),
                  heads, heads,
                  pl.BlockSpec((n_g, tm, LANE), lambda i: (0, i, 0)),
                  pl.BlockSpec((n_g, tm, LANE), lambda i: (1, i, 0)),
                  const(wna.shape), const(wsw.shape), const(wout.shape),
                  const((1, D_MODEL)), const((1, D_MODEL))],
        out_specs=pl.BlockSpec((tm, D_MODEL), lambda i: (i, 0)),
        compiler_params=pltpu.CompilerParams(dimension_semantics=("arbitrary",), vmem_limit_bytes=VMEM_LIMIT),
        name="post_attn",
    )(x2, lg, lb, oa, ob, gates, gates, wna, wsw, wout, g1, b1)


HALO = BF16_ROWS
FFN_SUB = MXU_COLS


_GELU_C = 2.0 * float(np.sqrt(2.0 / np.pi)) * float(np.log2(np.e))


def _gelu_tanh(x):
    u = x * (x * x * (-_GELU_C * 0.044715) - _GELU_C)
    return x / (1.0 + jnp.exp2(u))


def _ffn_kernel(h_ref, hn_ref, hm_ref, wg_ref, wv_ref, cwb_ref, wd_ref, ln_ref, o_ref, hb_ref, act_a, act_b,
                *, tm, tf, tiles_per_seq):
    i = pl.program_id(0)
    f = pl.program_id(1)
    nf = pl.num_programs(1) - 1
    pos = i % tiles_per_seq

    @pl.when(jnp.logical_and(f == 0, pos == 0))
    def _():
        hb_ref[0:HALO, :] = hm_ref[...].astype(BF16)

    @pl.when(jnp.logical_and(f == 0, pos != 0))
    def _():
        hb_ref[0:HALO, :] = hb_ref[tm:tm + HALO, :]

    @pl.when(f == 0)
    def _():
        nxt = jnp.where(pos == tiles_per_seq - 1, 0.0, hn_ref[...])
        hb_ref[HALO:HALO + tm, :] = h_ref[...].astype(BF16)
        hb_ref[HALO + tm:, :] = nxt.astype(BF16)
        o_ref[...] = jnp.zeros_like(o_ref)

    n = tm + 2 * HALO

    def up_stage(act_out):
        for c0 in range(0, tf, FFN_SUB):
            sl = slice(c0, c0 + FFN_SUB)
            gp = jnp.concatenate([jnp.dot(hb_ref[r0:r0 + n // 2, :], wg_ref[:, sl], preferred_element_type=F32)
                                  for r0 in (0, n // 2)], axis=0)
            vl = jnp.concatenate([jnp.dot(hb_ref[HALO + r0:HALO + r0 + tm // 2, :], wv_ref[:, sl],
                                          preferred_element_type=F32) for r0 in (0, tm // 2)], axis=0)
            up = pltpu.roll(gp, 1, 0)
            dn = pltpu.roll(gp, n - 1, 0)
            gate = (up * cwb_ref[0:1, sl] + gp * cwb_ref[1:2, sl] + dn * cwb_ref[2:3, sl])[HALO:HALO + tm] \
                + cwb_ref[3:4, sl]
            act_out[:, sl] = (_gelu_tanh(gate) * vl).astype(BF16)

    def down(act_in):
        return jnp.concatenate([jnp.dot(act_in[r0:r0 + tm // 2, :], wd_ref[...], preferred_element_type=F32)
                                for r0 in (0, tm // 2)], axis=0)

    @pl.when(f == 0)
    def _():
        up_stage(act_a)

    @pl.when(jnp.logical_and(jnp.logical_and(f > 0, f < nf), f % 2 == 1))
    def _():
        up_stage(act_b)
        o_ref[...] += down(act_a)

    @pl.when(jnp.logical_and(jnp.logical_and(f > 0, f < nf), f % 2 == 0))
    def _():
        up_stage(act_a)
        o_ref[...] += down(act_b)

    @pl.when(f == nf)
    def _():
        last = act_a if (D_FF // tf - 1) % 2 == 0 else act_b
        o_ref[...] = _ln_rows(ALPHA * h_ref[...] + (o_ref[...] + down(last)), ln_ref[0:1, :], ln_ref[1:2, :])


def _ffn(h, h_meta, w_in, cwb, wd, ln2, t, tm, tf):
    m = h.shape[0]
    nf = D_FF // tf
    tiles_per_seq = t // tm
    hb = tm // HALO
    last_hb = m // HALO - 1
    single = pl.Buffered(1)
    up_blk = lambda f: jnp.minimum(f, nf - 1)
    return pl.pallas_call(
        functools.partial(_ffn_kernel, tm=tm, tf=tf, tiles_per_seq=tiles_per_seq),
        out_shape=jax.ShapeDtypeStruct((m, D_MODEL), F32),
        grid=(m // tm, nf + 1),
        in_specs=[pl.BlockSpec((tm, D_MODEL), lambda i, f: (i, 0)),
                  pl.BlockSpec((HALO, D_MODEL), lambda i, f: (jnp.minimum((i + 1) * hb, last_hb), 0)),
                  pl.BlockSpec((N_META, D_MODEL), lambda i, f: (i // tiles_per_seq, 0)),
                  pl.BlockSpec((D_MODEL, tf), lambda i, f: (0, up_blk(f))),
                  pl.BlockSpec((D_MODEL, tf), lambda i, f: (0, up_blk(f) + nf)),
                  pl.BlockSpec((4, tf), lambda i, f: (0, up_blk(f))),
                  pl.BlockSpec((tf, D_MODEL), lambda i, f: (jnp.maximum(f - 1, 0), 0)),
                  pl.BlockSpec((2, D_MODEL), lambda i, f: (0, 0))],
        out_specs=pl.BlockSpec((tm, D_MODEL), lambda i, f: (i, 0), pipeline_mode=single),
        scratch_shapes=[pltpu.VMEM((tm + 2 * HALO, D_MODEL), BF16),
                        pltpu.VMEM((tm, tf), BF16), pltpu.VMEM((tm, tf), BF16)],
        compiler_params=pltpu.CompilerParams(
            dimension_semantics=("arbitrary", "arbitrary"), vmem_limit_bytes=VMEM_LIMIT),
        name="ffn",
    )(h, h, h_meta, w_in, w_in, cwb, wd, ln2)


def _layer_weights(w_in, w_proj_na, w_proj_sw, w_out, w_ffn_in, w_ffn_down):
    return (w_in[:, QKV_COLS:].astype(BF16), w_in[:, :QKV_COLS].astype(BF16),
            w_proj_na.astype(BF16), w_proj_sw.astype(BF16), w_out.astype(BF16),
            w_ffn_in.astype(BF16), w_ffn_down.astype(BF16))


def _trunk(x, meta_tokens, pm, gm, lg, lb, wts, tables, g1, b1, cwb, ln2):
    b_sz, t, _ = x.shape
    wgate, wqkv, wna, wsw, wout, wffn, wdown = wts
    na_bias, sw_bias, swm_bias = tables
    x2 = x.reshape(b_sz * t, D_MODEL)
    xn, gates = _ln_gates(x2, lg, lb, wgate, tm=512, parts=2)
    p = _qkv_proj(xn, wqkv, tm=1024)
    oa = _na_attn(p, pm, na_bias, b_sz, t)
    ob = _swa_attn(p, pm, sw_bias, b_sz, t, tq=2048)
    oa_m, ob_m = _meta_attn(p, pm, swm_bias, b_sz, t)
    h = _post_attn(x2, lg, lb, oa, ob, gates, wna, wsw, wout, g1, b1, tm=512, parts=2)
    x_m = jnp.tile(meta_tokens, (b_sz, 1))
    gates_m = jnp.tile(gm, (1, b_sz, 1))
    h_m = _post_attn(x_m, lg, lb, oa_m, ob_m, gates_m, wna, wsw, wout, g1, b1, tm=b_sz * N_META, parts=1)
    y = _ffn(h, h_m, wffn, cwb, wdown, ln2, t, tm=1024, tf=512)
    return y.reshape(b_sz, t, D_MODEL)


def kernel(x_prompt, x_sample, meta_tokens, ln_emb_g, ln_emb_b, w_in, na_rpb, sw_sink, w_proj_na, w_proj_sw, w_out, ln1_g, ln1_b, w_ffn_in, ffn_conv_w, ffn_conv_b, w_ffn_down, ln2_g, ln2_b):
    assert DEPTH == 1 and w_in.shape[0] == DEPTH
    row = lambda v: v.reshape(1, -1)
    lg, lb = row(ln_emb_g), row(ln_emb_b)
    wts = _layer_weights(w_in[0], w_proj_na[0], w_proj_sw[0], w_out[0], w_ffn_in[0], w_ffn_down[0])
    tables = (_na_bias(na_rpb[0]), _sw_bias(sw_sink[0]), _sw_meta_bias(sw_sink[0]))
    xn_m, gm = _ln_gates(meta_tokens, lg, lb, wts[0], tm=N_META, parts=1)
    pm = _qkv_proj(xn_m, wts[1], tm=N_META)
    cwb = jnp.concatenate([ffn_conv_w[0], row(ffn_conv_b[0])], axis=0)
    ln2 = jnp.stack([ln2_g[0], ln2_b[0]])
    args = (meta_tokens, pm, gm, lg, lb, wts, tables, row(ln1_g[0]), row(ln1_b[0]), cwb, ln2)
    return (_trunk(x_prompt, *args), _trunk(x_sample, *args))
```

```python
import functools

import numpy as np
import jax
import jax.numpy as jnp
from jax import lax
from jax.experimental import pallas as pl
from jax.experimental.pallas import tpu as pltpu

D_MODEL = 2048
N_META = 16
GRID_W = 64
NA_HEADS = 8
HEAD_DIM = 128
NA_WIN_ROWS = 8
NA_WIN_COLS = 16
SW_Q_HEADS = 8
SW_KV_HEADS = 2
SW_GROUP = SW_Q_HEADS // SW_KV_HEADS
SW_WINDOW = 128
SW_BLOCK = 128
D_FF = 5632
LN_EPS = 1e-5
NEG_INF = -1e30
DEPTH = 1
ALPHA = (2 * DEPTH) ** 0.25
SCALE = HEAD_DIM ** -0.5

LANE = 128
BF16_ROWS = 16
MXU_COLS = 256
VMEM_LIMIT = 56 * 1024 * 1024
FFN_VMEM_LIMIT = 60 * 1024 * 1024

SLAB_QA, SLAB_KA, SLAB_VA, SLAB_QB, SLAB_KB, SLAB_VB = 0, 8, 16, 24, 32, 34
QKV_COLS = 4608
QKV_SLABS = QKV_COLS // LANE
GATE_COLS = 2 * D_MODEL
GATE_SLABS = GATE_COLS // LANE
PROJ_CHUNK = 2 * MXU_COLS

NA_QROWS = 4
NA_WROWS = 11
NA_QBLK = NA_QROWS * GRID_W
NA_WKEYS = NA_WROWS * GRID_W
NA_KEYS = 768
SW_WKEYS = 3 * SW_BLOCK
SW_KEYS = 512
SW_MKEYS = 256
ATTN_UNROLL = 4

F32 = jnp.float32
BF16 = jnp.bfloat16


def _ln_rows(x, g, b):
    mu = jnp.mean(x, axis=-1, keepdims=True)
    xc = x - mu
    var = jnp.mean(xc * xc, axis=-1, keepdims=True)
    return xc * lax.rsqrt(var + LN_EPS) * g + b


def _scores(q, kcat, bias):
    s = lax.dot_general(q, kcat, (((1,), (1,)), ((), ())), preferred_element_type=F32)
    s = s * SCALE
    return s if bias is None else s + bias


def _softmax_pv_scores(s, vcat):
    m = jnp.max(s, axis=-1, keepdims=True)
    e = jnp.exp(s - m)
    l = jnp.sum(e, axis=-1, keepdims=True)
    o = jnp.dot(e.astype(BF16), vcat, preferred_element_type=F32)
    return o / l


def _softmax_pv(q, kcat, vcat, bias):
    return _softmax_pv_scores(_scores(q, kcat, bias), vcat)


def _pipelined_blocks(n_blocks, scores, finish):
    scores(jnp.int32(0), 0)

    def body(it, carry):
        for u in range(ATTN_UNROLL):
            i = it * ATTN_UNROLL + u
            scores(jnp.minimum(i + 1, n_blocks - 1), (u + 1) % ATTN_UNROLL)
            finish(i, u)
        return carry

    lax.fori_loop(0, n_blocks // ATTN_UNROLL, body, 0)


def _project_to_slabs(xn, w_ref, o_ref, rows):
    for c0 in range(0, w_ref.shape[1], PROJ_CHUNK):
        r = jnp.dot(xn, w_ref[:, c0:c0 + PROJ_CHUNK], preferred_element_type=F32)
        for s in range(PROJ_CHUNK // LANE):
            o_ref[c0 // LANE + s, rows, :] = r[:, s * LANE:(s + 1) * LANE].astype(BF16)


def _ln_gates_kernel(x_ref, g_ref, b_ref, w_ref, xn_ref, o_ref, *, parts):
    hr = x_ref.shape[0] // parts
    for r in range(parts):
        rows = slice(r * hr, (r + 1) * hr)
        xn = _ln_rows(x_ref[rows, :], g_ref[...], b_ref[...]).astype(BF16)
        xn_ref[rows, :] = xn
        _project_to_slabs(xn, w_ref, o_ref, rows)


def _ln_gates(x2, g, b, w, tm, parts):
    m = x2.shape[0]
    const = lambda shape: pl.BlockSpec(shape, lambda i: (0,) * len(shape), pipeline_mode=pl.Buffered(1))
    return pl.pallas_call(
        functools.partial(_ln_gates_kernel, parts=parts),
        out_shape=(jax.ShapeDtypeStruct((m, D_MODEL), BF16),
                   jax.ShapeDtypeStruct((GATE_SLABS, m, LANE), BF16)),
        grid=(m // tm,),
        in_specs=[pl.BlockSpec((tm, D_MODEL), lambda i: (i, 0)),
                  const((1, D_MODEL)), const((1, D_MODEL)), const(w.shape)],
        out_specs=(pl.BlockSpec((tm, D_MODEL), lambda i: (i, 0)),
                   pl.BlockSpec((GATE_SLABS, tm, LANE), lambda i: (0, i, 0))),
        compiler_params=pltpu.CompilerParams(dimension_semantics=("arbitrary",), vmem_limit_bytes=VMEM_LIMIT),
        name="ln_gates",
    )(x2, g, b, w)


def _qkv_kernel(xn_ref, w_ref, o_ref):
    _project_to_slabs(xn_ref[...], w_ref, o_ref, slice(None))


def _qkv_proj(xn, w, tm):
    m = xn.shape[0]
    return pl.pallas_call(
        _qkv_kernel,
        out_shape=jax.ShapeDtypeStruct((QKV_SLABS, m, LANE), BF16),
        grid=(m // tm,),
        in_specs=[pl.BlockSpec((tm, D_MODEL), lambda i: (i, 0)),
                  pl.BlockSpec(w.shape, lambda i: (0, 0), pipeline_mode=pl.Buffered(1))],
        out_specs=pl.BlockSpec((QKV_SLABS, tm, LANE), lambda i: (0, i, 0)),
        compiler_params=pltpu.CompilerParams(dimension_semantics=("arbitrary",), vmem_limit_bytes=VMEM_LIMIT),
        name="qkv_proj",
    )(xn, w)


def _na_kernel(q_ref, k_ref, v_ref, km_ref, vm_ref, bias_ref, o_ref, kcat_ref, vcat_ref, s_ref, *, rows, nblk):
    pad = jnp.zeros((NA_KEYS - NA_WKEYS - N_META, HEAD_DIM), BF16)
    for u in range(ATTN_UNROLL):
        kcat_ref[u, NA_WKEYS:NA_WKEYS + N_META, :] = km_ref[0]
        vcat_ref[u, NA_WKEYS:NA_WKEYS + N_META, :] = vm_ref[0]
        kcat_ref[u, NA_WKEYS + N_META:, :] = pad
        vcat_ref[u, NA_WKEYS + N_META:, :] = pad

    def window_start(i):
        r0 = i * NA_QROWS
        w0 = jnp.minimum(jnp.clip(r0 - NA_WIN_ROWS // 2, 0, rows - NA_WIN_ROWS), rows - NA_WROWS)
        return pl.multiple_of(w0 * GRID_W, GRID_W)

    def scores(i, slot):
        kcat_ref[slot, 0:NA_WKEYS, :] = k_ref[0, pl.ds(window_start(i), NA_WKEYS), :]
        variant = jnp.where(i == 0, 0, jnp.where(i == nblk - 1, 2, 1))
        q = q_ref[0, pl.ds(pl.multiple_of(i * NA_QBLK, NA_QBLK), NA_QBLK), :]
        s_ref[slot] = _scores(q, kcat_ref[slot], bias_ref[variant, 0])

    def finish(i, slot):
        vcat_ref[slot, 0:NA_WKEYS, :] = v_ref[0, pl.ds(window_start(i), NA_WKEYS), :]
        o = _softmax_pv_scores(s_ref[slot], vcat_ref[slot])
        o_ref[0, pl.ds(pl.multiple_of(i * NA_QBLK, NA_QBLK), NA_QBLK), :] = o.astype(BF16)

    _pipelined_blocks(nblk, scores, finish)


def _na_attn(p, pm, bias, b_sz, t):
    rows = t // GRID_W
    nblk = rows // NA_QROWS
    assert rows % NA_QROWS == 0 and nblk >= 3 and rows >= NA_WROWS and nblk % ATTN_UNROLL == 0
    seq = lambda slab: pl.BlockSpec((1, t, HEAD_DIM), lambda b, h: (slab + h, b, 0))
    meta = lambda slab: pl.BlockSpec((1, N_META, HEAD_DIM), lambda b, h: (slab + h, 0, 0))
    return pl.pallas_call(
        functools.partial(_na_kernel, rows=rows, nblk=nblk),
        out_shape=jax.ShapeDtypeStruct((NA_HEADS, b_sz * t, HEAD_DIM), BF16),
        grid=(b_sz, NA_HEADS),
        in_specs=[seq(SLAB_QA), seq(SLAB_KA), seq(SLAB_VA), meta(SLAB_KA), meta(SLAB_VA),
                  pl.BlockSpec((3, 1, NA_QBLK, NA_KEYS), lambda b, h: (0, h, 0, 0))],
        out_specs=pl.BlockSpec((1, t, HEAD_DIM), lambda b, h: (h, b, 0)),
        scratch_shapes=[pltpu.VMEM((ATTN_UNROLL, NA_KEYS, HEAD_DIM), BF16)] * 2
        + [pltpu.VMEM((ATTN_UNROLL, NA_QBLK, NA_KEYS), F32)],
        compiler_params=pltpu.CompilerParams(
            dimension_semantics=("arbitrary", "arbitrary"), vmem_limit_bytes=VMEM_LIMIT),
        name="na_attn",
    )(p, p, p, pm, pm, bias)


def _na_bias(rpb, dtype=F32):
    edge = GRID_W - NA_WIN_COLS
    ext = jnp.pad(rpb.astype(dtype), ((0, 0), (0, 0), (edge, edge + 1)), mode="edge")
    n_r, span = ext.shape[1], 2 * GRID_W
    stream = jnp.broadcast_to(ext[:, :, None, :], (NA_HEADS, n_r, GRID_W, span)).reshape(NA_HEADS, n_r, GRID_W * span)
    skew = stream[:, :, :GRID_W * (span - 1)].reshape(NA_HEADS, n_r, GRID_W, span - 1)
    cols = skew[:, :, :, GRID_W - 1:2 * GRID_W - 1]
    rpad = NA_WROWS - NA_WIN_ROWS
    cols = jnp.pad(cols, ((0, 0), (rpad, rpad), (0, 0), (0, 0)))
    qc = np.arange(GRID_W)[:, None, None]
    j = np.arange(NA_WROWS)[None, :, None]
    kc = np.arange(GRID_W)[None, None, :]
    col_start = np.clip(qc - NA_WIN_COLS // 2, 0, GRID_W - NA_WIN_COLS)
    in_cols = (kc >= col_start) & (kc < col_start + NA_WIN_COLS)
    tables = []
    variants = ([(i, 0) for i in range(NA_QROWS)],
                [(i + NA_WIN_ROWS // 2, i) for i in range(NA_QROWS)],
                [(i + NA_WROWS - NA_QROWS, NA_WROWS - NA_WIN_ROWS) for i in range(NA_QROWS)])
    for variant in variants:
        per_row = []
        for qoff, rs in variant:
            r_lo = (NA_WIN_ROWS - 1) - qoff + rpad
            blk = jnp.transpose(cols[:, r_lo:r_lo + NA_WROWS], (0, 2, 1, 3))
            mask = ((j >= rs) & (j < rs + NA_WIN_ROWS)) & in_cols
            per_row.append(jnp.where(mask[None], blk, NEG_INF))
        tables.append(jnp.stack(per_row, axis=1).reshape(NA_HEADS, NA_QBLK, NA_WKEYS))
    tab = jnp.stack(tables)
    lead = tab.shape[:-1]
    return jnp.concatenate([tab, jnp.zeros(lead + (N_META,), dtype),
                            jnp.full(lead + (NA_KEYS - NA_WKEYS - N_META,), NEG_INF, dtype)], axis=-1)


def _swa_kernel(q_ref, k_ref, v_ref, km_ref, vm_ref, bias_ref, o_ref, kcat_ref, vcat_ref, s_ref, *, t, nb, bps):
    pad = jnp.zeros((SW_KEYS - SW_WKEYS - N_META, HEAD_DIM), BF16)
    for u in range(ATTN_UNROLL):
        kcat_ref[u, SW_WKEYS:SW_WKEYS + N_META, :] = km_ref[0]
        vcat_ref[u, SW_WKEYS:SW_WKEYS + N_META, :] = vm_ref[0]
        kcat_ref[u, SW_WKEYS + N_META:, :] = pad
        vcat_ref[u, SW_WKEYS + N_META:, :] = pad
    step = pl.program_id(2)

    def window_start(i):
        n = step * bps + i
        return pl.multiple_of(jnp.clip((n - 1) * SW_BLOCK, 0, t - SW_WKEYS), SW_BLOCK)

    def scores(i, slot):
        n = step * bps + i
        kcat_ref[slot, 0:SW_WKEYS, :] = k_ref[0, pl.ds(window_start(i), SW_WKEYS), :]
        variant = jnp.where(n == 0, 0, jnp.where(n == nb - 1, 2, 1))
        q = q_ref[:, pl.ds(pl.multiple_of(i * SW_BLOCK, SW_BLOCK), SW_BLOCK), :]
        bias = bias_ref[variant].reshape(SW_GROUP * SW_BLOCK, SW_KEYS)
        s_ref[slot] = _scores(q.reshape(SW_GROUP * SW_BLOCK, HEAD_DIM), kcat_ref[slot], bias)

    def finish(i, slot):
        vcat_ref[slot, 0:SW_WKEYS, :] = v_ref[0, pl.ds(window_start(i), SW_WKEYS), :]
        o = _softmax_pv_scores(s_ref[slot], vcat_ref[slot])
        qs = pl.multiple_of(i * SW_BLOCK, SW_BLOCK)
        o_ref[:, pl.ds(qs, SW_BLOCK), :] = o.reshape(SW_GROUP, SW_BLOCK, HEAD_DIM).astype(BF16)

    _pipelined_blocks(bps, scores, finish)


def _swa_attn(p, pm, bias, b_sz, t, tq):
    nb = t // SW_BLOCK
    assert nb >= 3 and t % tq == 0 and (tq // SW_BLOCK) % ATTN_UNROLL == 0
    steps = t // tq
    bps = tq // SW_BLOCK
    kv = lambda slab: pl.BlockSpec((1, t, HEAD_DIM), lambda b, g, s: (slab + g, b, 0))
    meta = lambda slab: pl.BlockSpec((1, N_META, HEAD_DIM), lambda b, g, s: (slab + g, 0, 0))
    return pl.pallas_call(
        functools.partial(_swa_kernel, t=t, nb=nb, bps=bps),
        out_shape=jax.ShapeDtypeStruct((SW_Q_HEADS, b_sz * t, HEAD_DIM), BF16),
        grid=(b_sz, SW_KV_HEADS, steps),
        in_specs=[pl.BlockSpec((SW_GROUP, tq, HEAD_DIM), lambda b, g, s: (SLAB_QB // SW_GROUP + g, b * steps + s, 0)),
                  kv(SLAB_KB), kv(SLAB_VB), meta(SLAB_KB), meta(SLAB_VB),
                  pl.BlockSpec((3, SW_GROUP, SW_BLOCK, SW_KEYS), lambda b, g, s: (0, g, 0, 0))],
        out_specs=pl.BlockSpec((SW_GROUP, tq, HEAD_DIM), lambda b, g, s: (g, b * steps + s, 0)),
        scratch_shapes=[pltpu.VMEM((ATTN_UNROLL, SW_KEYS, HEAD_DIM), BF16)] * 2
        + [pltpu.VMEM((ATTN_UNROLL, SW_GROUP * SW_BLOCK, SW_KEYS), F32)],
        compiler_params=pltpu.CompilerParams(
            dimension_semantics=("arbitrary", "arbitrary", "arbitrary"), vmem_limit_bytes=VMEM_LIMIT),
        name="swa_attn",
    )(p, p, p, pm, pm, bias)


def _sw_slopes():
    return np.power(2.0, -8.0 * np.arange(1, SW_Q_HEADS + 1, dtype=np.float64) / SW_Q_HEADS).astype(np.float32)


def _sw_bias(sink, dtype=F32):
    i = np.arange(SW_BLOCK)[:, None]
    j = np.arange(SW_WKEYS)[None, :]
    slopes = jnp.asarray(_sw_slopes(), dtype)[:, None, None]
    tables = []
    for qo in (0, SW_BLOCK, 2 * SW_BLOCK):
        dist = np.abs(qo + i - j)
        pen = -(jnp.asarray(dist, dtype)[None] * slopes)
        tables.append(jnp.where((dist <= SW_WINDOW)[None], pen, NEG_INF))
    tab = jnp.stack(tables)
    lead = tab.shape[:-1]
    snk = jnp.broadcast_to(sink.astype(dtype)[None, :, None, None], lead + (1,))
    return jnp.concatenate([tab, jnp.zeros(lead + (N_META,), dtype), snk,
                            jnp.full(lead + (SW_KEYS - SW_WKEYS - N_META - 1,), NEG_INF, dtype)], axis=-1)


def _sw_meta_bias(sink, dtype=F32):
    q = np.arange(N_META)[:, None]
    k = np.arange(SW_BLOCK)[None, :]
    dist = (N_META + k) - q
    slopes = jnp.asarray(_sw_slopes(), dtype)[:, None, None]
    pen = jnp.where((dist <= SW_WINDOW)[None], -(slopes * jnp.asarray(dist, dtype)[None]), NEG_INF)
    lead = pen.shape[:-1]
    snk = jnp.broadcast_to(sink.astype(dtype)[:, None, None], lead + (1,))
    return jnp.concatenate([jnp.zeros(lead + (N_META,), dtype), pen, snk,
                            jnp.full(lead + (SW_MKEYS - N_META - SW_BLOCK - 1,), NEG_INF, dtype)], axis=-1)


def _meta_attn_kernel(pm_ref, k0_ref, v0_ref, bias_ref, oa_ref, ob_ref, kcat_ref, vcat_ref):
    for h in range(NA_HEADS):
        o = _softmax_pv(pm_ref[SLAB_QA + h], pm_ref[SLAB_KA + h], pm_ref[SLAB_VA + h], None)
        oa_ref[h] = o.astype(BF16)
    pad = jnp.zeros((SW_MKEYS - N_META - SW_BLOCK, HEAD_DIM), BF16)
    for g in range(SW_KV_HEADS):
        kcat_ref[0:N_META, :] = pm_ref[SLAB_KB + g]
        vcat_ref[0:N_META, :] = pm_ref[SLAB_VB + g]
        kcat_ref[N_META:N_META + SW_BLOCK, :] = k0_ref[g]
        vcat_ref[N_META:N_META + SW_BLOCK, :] = v0_ref[g]
        kcat_ref[N_META + SW_BLOCK:, :] = pad
        vcat_ref[N_META + SW_BLOCK:, :] = pad
        q = jnp.concatenate([pm_ref[SLAB_QB + g * SW_GROUP + r] for r in range(SW_GROUP)], axis=0)
        bias = bias_ref[g * SW_GROUP:(g + 1) * SW_GROUP].reshape(SW_GROUP * N_META, SW_MKEYS)
        o = _softmax_pv(q, kcat_ref[...], vcat_ref[...], bias)
        for r in range(SW_GROUP):
            ob_ref[g * SW_GROUP + r] = o[r * N_META:(r + 1) * N_META].astype(BF16)


def _meta_attn(p, pm, bias, b_sz, t):
    blocks_per_seq = t // SW_BLOCK
    first_block = lambda slab: pl.BlockSpec((SW_KV_HEADS, SW_BLOCK, HEAD_DIM),
                                            lambda b: (slab // SW_KV_HEADS, b * blocks_per_seq, 0))
    out = jax.ShapeDtypeStruct((NA_HEADS, b_sz * N_META, HEAD_DIM), BF16)
    return pl.pallas_call(
        _meta_attn_kernel,
        out_shape=(out, out),
        grid=(b_sz,),
        in_specs=[pl.BlockSpec((QKV_SLABS, N_META, HEAD_DIM), lambda b: (0, 0, 0)),
                  first_block(SLAB_KB), first_block(SLAB_VB),
                  pl.BlockSpec((SW_Q_HEADS, N_META, SW_MKEYS), lambda b: (0, 0, 0))],
        out_specs=(pl.BlockSpec((NA_HEADS, N_META, HEAD_DIM), lambda b: (0, b, 0)),
                   pl.BlockSpec((SW_Q_HEADS, N_META, HEAD_DIM), lambda b: (0, b, 0))),
        scratch_shapes=[pltpu.VMEM((SW_MKEYS, HEAD_DIM), BF16), pltpu.VMEM((SW_MKEYS, HEAD_DIM), BF16)],
        compiler_params=pltpu.CompilerParams(dimension_semantics=("arbitrary",)),
        name="meta_attn",
    )(pm, p, p, bias)


def _post_kernel(x_ref, lg_ref, lb_ref, oa_ref, ob_ref, ga_ref, gb_ref, wna_ref, wsw_ref, wout_ref,
                 g1_ref, b1_ref, h_ref, *, parts):
    hr = x_ref.shape[0] // parts
    for r in range(parts):
        rows = slice(r * hr, (r + 1) * hr)
        oa = jnp.concatenate([oa_ref[h, rows, :] for h in range(NA_HEADS)], axis=1)
        ob = jnp.concatenate([ob_ref[h, rows, :] for h in range(SW_Q_HEADS)], axis=1)
        merged = []
        for c0 in range(0, D_MODEL, PROJ_CHUNK):
            cols = slice(c0, c0 + PROJ_CHUNK)
            slabs = range(c0 // LANE, (c0 + PROJ_CHUNK) // LANE)
            a = jnp.dot(oa, wna_ref[:, cols], preferred_element_type=F32)
            b = jnp.dot(ob, wsw_ref[:, cols], preferred_element_type=F32)
            ga = jnp.concatenate([ga_ref[s, rows, :] for s in slabs], axis=1).astype(F32)
            gb = jnp.concatenate([gb_ref[s, rows, :] for s in slabs], axis=1).astype(F32)
            merged.append((jax.nn.sigmoid(ga) * a + jax.nn.sigmoid(gb) * b).astype(BF16))
        merged = jnp.concatenate(merged, axis=1)
        y = jnp.dot(merged, wout_ref[...], preferred_element_type=F32)
        h0 = _ln_rows(x_ref[rows, :], lg_ref[...], lb_ref[...])
        h_ref[rows, :] = _ln_rows(ALPHA * h0 + y, g1_ref[...], b1_ref[...])


def _post_attn(x2, lg, lb, oa, ob, gates, wna, wsw, wout, g1, b1, tm, parts):
    m = x2.shape[0]
    n_g = D_MODEL // LANE
    const = lambda shape: pl.BlockSpec(shape, lambda i: (0,) * len(shape), pipeline_mode=pl.Buffered(1))
    heads = pl.BlockSpec((NA_HEADS, tm, HEAD_DIM), lambda i: (0, i, 0))
    return pl.pallas_call(
        functools.partial(_post_kernel, parts=parts),
        out_shape=jax.ShapeDtypeStruct((m, D_MODEL), F32),
        grid=(m // tm,),
        in_specs=[pl.BlockSpec((tm, D_MODEL), lambda i: (i, 0)),
                  const((1, D_MODEL)), const((1, D_MODEL)),
                  heads, heads,
                  pl.BlockSpec((n_g, tm, LANE), lambda i: (0, i, 0)),
                  pl.BlockSpec((n_g, tm, LANE), lambda i: (1, i, 0)),
                  const(wna.shape), const(wsw.shape), const(wout.shape),
                  const((1, D_MODEL)), const((1, D_MODEL))],
        out_specs=pl.BlockSpec((tm, D_MODEL), lambda i: (i, 0)),
        compiler_params=pltpu.CompilerParams(dimension_semantics=("arbitrary",), vmem_limit_bytes=VMEM_LIMIT),
        name="post_attn",
    )(x2, lg, lb, oa, ob, gates, gates, wna, wsw, wout, g1, b1)


HALO = BF16_ROWS
FFN_SUB = MXU_COLS


_GELU_C = 2.0 * float(np.sqrt(2.0 / np.pi)) * float(np.log2(np.e))


def _gelu_tanh(x):
    u = x * (x * x * (-_GELU_C * 0.044715) - _GELU_C)
    return x / (1.0 + jnp.exp2(u))


def _ffn_kernel(h_ref, hn_ref, hm_ref, wg_ref, wv_ref, cwb_ref, wd_ref, ln_ref, o_ref, hb_ref, act_a, act_b,
                *, tm, tf, tiles_per_seq):
    i = pl.program_id(0)
    f = pl.program_id(1)
    nf = pl.num_programs(1) - 1
    pos = i % tiles_per_seq

    @pl.when(jnp.logical_and(f == 0, pos == 0))
    def _():
        hb_ref[0:HALO, :] = hm_ref[...].astype(BF16)

    @pl.when(jnp.logical_and(f == 0, pos != 0))
    def _():
        hb_ref[0:HALO, :] = hb_ref[tm:tm + HALO, :]

    @pl.when(f == 0)
    def _():
        nxt = jnp.where(pos == tiles_per_seq - 1, 0.0, hn_ref[...])
        hb_ref[HALO:HALO + tm, :] = h_ref[...].astype(BF16)
        hb_ref[HALO + tm:, :] = nxt.astype(BF16)
        o_ref[...] = jnp.zeros_like(o_ref)

    n = tm + 2 * HALO

    def up_stage(act_out):
        for c0 in range(0, tf, FFN_SUB):
            sl = slice(c0, c0 + FFN_SUB)
            gp = jnp.concatenate([jnp.dot(hb_ref[r0:r0 + n // 2, :], wg_ref[:, sl], preferred_element_type=F32)
                                  for r0 in (0, n // 2)], axis=0)
            vl = jnp.concatenate([jnp.dot(hb_ref[HALO + r0:HALO + r0 + tm // 2, :], wv_ref[:, sl],
                                          preferred_element_type=F32) for r0 in (0, tm // 2)], axis=0)
            up = pltpu.roll(gp, 1, 0)
            dn = pltpu.roll(gp, n - 1, 0)
            gate = (up * cwb_ref[0:1, sl] + gp * cwb_ref[1:2, sl] + dn * cwb_ref[2:3, sl])[HALO:HALO + tm] \
                + cwb_ref[3:4, sl]
            act_out[:, sl] = (_gelu_tanh(gate) * vl).astype(BF16)

    def down(act_in):
        return jnp.concatenate([jnp.dot(act_in[r0:r0 + tm // 2, :], wd_ref[...], preferred_element_type=F32)
                                for r0 in (0, tm // 2)], axis=0)

    @pl.when(f == 0)
    def _():
        up_stage(act_a)

    @pl.when(jnp.logical_and(jnp.logical_and(f > 0, f < nf), f % 2 == 1))
    def _():
        up_stage(act_b)
        o_ref[...] += down(act_a)

    @pl.when(jnp.logical_and(jnp.logical_and(f > 0, f < nf), f % 2 == 0))
    def _():
        up_stage(act_a)
        o_ref[...] += down(act_b)

    @pl.when(f == nf)
    def _():
        last = act_a if (D_FF // tf - 1) % 2 == 0 else act_b
        o_ref[...] = _ln_rows(ALPHA * h_ref[...] + (o_ref[...] + down(last)), ln_ref[0:1, :], ln_ref[1:2, :])


def _ffn(h, h_meta, w_in, cwb, wd, ln2, t, tm, tf):
    m = h.shape[0]
    nf = D_FF // tf
    tiles_per_seq = t // tm
    hb = tm // HALO
    last_hb = m // HALO - 1
    up_blk = lambda f: jnp.minimum(f, nf - 1)
    return pl.pallas_call(
        functools.partial(_ffn_kernel, tm=tm, tf=tf, tiles_per_seq=tiles_per_seq),
        out_shape=jax.ShapeDtypeStruct((m, D_MODEL), F32),
        grid=(m // tm, nf + 1),
        in_specs=[pl.BlockSpec((tm, D_MODEL), lambda i, f: (i, 0)),
                  pl.BlockSpec((HALO, D_MODEL), lambda i, f: (jnp.minimum((i + 1) * hb, last_hb), 0)),
                  pl.BlockSpec((N_META, D_MODEL), lambda i, f: (i // tiles_per_seq, 0)),
                  pl.BlockSpec((D_MODEL, tf), lambda i, f: (0, up_blk(f))),
                  pl.BlockSpec((D_MODEL, tf), lambda i, f: (0, up_blk(f) + nf)),
                  pl.BlockSpec((4, tf), lambda i, f: (0, up_blk(f))),
                  pl.BlockSpec((tf, D_MODEL), lambda i, f: (jnp.maximum(f - 1, 0), 0)),
                  pl.BlockSpec((2, D_MODEL), lambda i, f: (0, 0))],
        out_specs=pl.BlockSpec((tm, D_MODEL), lambda i, f: (i, 0)),
        scratch_shapes=[pltpu.VMEM((tm + 2 * HALO, D_MODEL), BF16),
                        pltpu.VMEM((tm, tf), BF16), pltpu.VMEM((tm, tf), BF16)],
        compiler_params=pltpu.CompilerParams(
            dimension_semantics=("arbitrary", "arbitrary"), vmem_limit_bytes=FFN_VMEM_LIMIT),
        name="ffn",
    )(h, h, h_meta, w_in, w_in, cwb, wd, ln2)


def _layer_weights(w_in, w_proj_na, w_proj_sw, w_out, w_ffn_in, w_ffn_down):
    return (w_in[:, QKV_COLS:].astype(BF16), w_in[:, :QKV_COLS].astype(BF16),
            w_proj_na.astype(BF16), w_proj_sw.astype(BF16), w_out.astype(BF16),
            w_ffn_in.astype(BF16), w_ffn_down.astype(BF16))


def _trunk(x, meta_tokens, pm, gm, lg, lb, wts, tables, g1, b1, cwb, ln2):
    b_sz, t, _ = x.shape
    wgate, wqkv, wna, wsw, wout, wffn, wdown = wts
    na_bias, sw_bias, swm_bias = tables
    x2 = x.reshape(b_sz * t, D_MODEL)
    xn, gates = _ln_gates(x2, lg, lb, wgate, tm=512, parts=2)
    p = _qkv_proj(xn, wqkv, tm=1024)
    oa = _na_attn(p, pm, na_bias, b_sz, t)
    ob = _swa_attn(p, pm, sw_bias, b_sz, t, tq=2048)
    oa_m, ob_m = _meta_attn(p, pm, swm_bias, b_sz, t)
    h = _post_attn(x2, lg, lb, oa, ob, gates, wna, wsw, wout, g1, b1, tm=512, parts=2)
    x_m = jnp.tile(meta_tokens, (b_sz, 1))
    gates_m = jnp.tile(gm, (1, b_sz, 1))
    h_m = _post_attn(x_m, lg, lb, oa_m, ob_m, gates_m, wna, wsw, wout, g1, b1, tm=b_sz * N_META, parts=1)
    y = _ffn(h, h_m, wffn, cwb, wdown, ln2, t, tm=1024, tf=512)
    return y.reshape(b_sz, t, D_MODEL)


def kernel(x_prompt, x_sample, meta_tokens, ln_emb_g, ln_emb_b, w_in, na_rpb, sw_sink, w_proj_na, w_proj_sw, w_out, ln1_g, ln1_b, w_ffn_in, ffn_conv_w, ffn_conv_b, w_ffn_down, ln2_g, ln2_b):
    assert DEPTH == 1 and w_in.shape[0] == DEPTH
    row = lambda v: v.reshape(1, -1)
    lg, lb = row(ln_emb_g), row(ln_emb_b)
    wts = _layer_weights(w_in[0], w_proj_na[0], w_proj_sw[0], w_out[0], w_ffn_in[0], w_ffn_down[0])
    tables = (_na_bias(na_rpb[0]), _sw_bias(sw_sink[0]), _sw_meta_bias(sw_sink[0]))
    xn_m, gm = _ln_gates(meta_tokens, lg, lb, wts[0], tm=N_META, parts=1)
    pm = _qkv_proj(xn_m, wts[1], tm=N_META)
    cwb = jnp.concatenate([ffn_conv_w[0], row(ffn_conv_b[0])], axis=0)
    ln2 = jnp.stack([ln2_g[0], ln2_b[0]])
    args = (meta_tokens, pm, gm, lg, lb, wts, tables, row(ln1_g[0]), row(ln1_b[0]), cwb, ln2)
    return (_trunk(x_prompt, *args), _trunk(x_sample, *args))
```

```python
import functools

import numpy as np
import jax
import jax.numpy as jnp
from jax import lax
from jax.experimental import pallas as pl
from jax.experimental.pallas import tpu as pltpu

D_MODEL = 2048
N_META = 16
GRID_W = 64
NA_HEADS = 8
HEAD_DIM = 128
NA_WIN_ROWS = 8
NA_WIN_COLS = 16
SW_Q_HEADS = 8
SW_KV_HEADS = 2
SW_GROUP = SW_Q_HEADS // SW_KV_HEADS
SW_WINDOW = 128
SW_BLOCK = 128
D_FF = 5632
LN_EPS = 1e-5
NEG_INF = -1e30
DEPTH = 1
ALPHA = (2 * DEPTH) ** 0.25
SCALE = HEAD_DIM ** -0.5

LANE = 128
BF16_ROWS = 16
MXU_COLS = 256
VMEM_LIMIT = 56 * 1024 * 1024
FFN_VMEM_LIMIT = 60 * 1024 * 1024

SLAB_QA, SLAB_KA, SLAB_VA, SLAB_QB, SLAB_KB, SLAB_VB = 0, 8, 16, 24, 32, 34
QKV_COLS = 4608
QKV_SLABS = QKV_COLS // LANE
GATE_COLS = 2 * D_MODEL
GATE_SLABS = GATE_COLS // LANE
PROJ_CHUNK = 2 * MXU_COLS

NA_QROWS = 4
NA_WROWS = 11
NA_QBLK = NA_QROWS * GRID_W
NA_WKEYS = NA_WROWS * GRID_W
NA_KEYS = 768
SW_WKEYS = 3 * SW_BLOCK
SW_KEYS = 512
SW_MKEYS = 256
ATTN_UNROLL = 4

F32 = jnp.float32
BF16 = jnp.bfloat16


def _ln_rows(x, g, b):
    mu = jnp.mean(x, axis=-1, keepdims=True)
    xc = x - mu
    var = jnp.mean(xc * xc, axis=-1, keepdims=True)
    return xc * lax.rsqrt(var + LN_EPS) * g + b


def _scores(q, kcat, bias):
    s = lax.dot_general(q, kcat, (((1,), (1,)), ((), ())), preferred_element_type=F32)
    s = s * SCALE
    return s if bias is None else s + bias


def _softmax_pv_scores(s, vcat):
    m = jnp.max(s, axis=-1, keepdims=True)
    e = jnp.exp(s - m)
    l = jnp.sum(e, axis=-1, keepdims=True)
    o = jnp.dot(e.astype(BF16), vcat, preferred_element_type=F32)
    return o / l


def _softmax_pv(q, kcat, vcat, bias):
    return _softmax_pv_scores(_scores(q, kcat, bias), vcat)


def _pipelined_blocks(n_blocks, scores, finish):
    scores(jnp.int32(0), 0)

    def body(it, carry):
        for u in range(ATTN_UNROLL):
            i = it * ATTN_UNROLL + u
            scores(jnp.minimum(i + 1, n_blocks - 1), (u + 1) % ATTN_UNROLL)
            finish(i, u)
        return carry

    lax.fori_loop(0, n_blocks // ATTN_UNROLL, body, 0)


def _project_to_slabs(xn, w_ref, o_ref, rows):
    for c0 in range(0, w_ref.shape[1], PROJ_CHUNK):
        r = jnp.dot(xn, w_ref[:, c0:c0 + PROJ_CHUNK], preferred_element_type=F32)
        for s in range(PROJ_CHUNK // LANE):
            o_ref[c0 // LANE + s, rows, :] = r[:, s * LANE:(s + 1) * LANE].astype(BF16)


def _ln_gates_kernel(x_ref, g_ref, b_ref, w_ref, xn_ref, o_ref, *, parts):
    hr = x_ref.shape[0] // parts
    for r in range(parts):
        rows = slice(r * hr, (r + 1) * hr)
        xn = _ln_rows(x_ref[rows, :], g_ref[...], b_ref[...]).astype(BF16)
        xn_ref[rows, :] = xn
        _project_to_slabs(xn, w_ref, o_ref, rows)


def _ln_gates(x2, g, b, w, tm, parts):
    m = x2.shape[0]
    const = lambda shape: pl.BlockSpec(shape, lambda i: (0,) * len(shape), pipeline_mode=pl.Buffered(1))
    return pl.pallas_call(
        functools.partial(_ln_gates_kernel, parts=parts),
        out_shape=(jax.ShapeDtypeStruct((m, D_MODEL), BF16),
                   jax.ShapeDtypeStruct((GATE_SLABS, m, LANE), BF16)),
        grid=(m // tm,),
        in_specs=[pl.BlockSpec((tm, D_MODEL), lambda i: (i, 0)),
                  const((1, D_MODEL)), const((1, D_MODEL)), const(w.shape)],
        out_specs=(pl.BlockSpec((tm, D_MODEL), lambda i: (i, 0)),
                   pl.BlockSpec((GATE_SLABS, tm, LANE), lambda i: (0, i, 0))),
        compiler_params=pltpu.CompilerParams(dimension_semantics=("arbitrary",), vmem_limit_bytes=VMEM_LIMIT),
        name="ln_gates",
    )(x2, g, b, w)


def _qkv_kernel(xn_ref, w_ref, o_ref):
    _project_to_slabs(xn_ref[...], w_ref, o_ref, slice(None))


def _qkv_proj(xn, w, tm):
    m = xn.shape[0]
    return pl.pallas_call(
        _qkv_kernel,
        out_shape=jax.ShapeDtypeStruct((QKV_SLABS, m, LANE), BF16),
        grid=(m // tm,),
        in_specs=[pl.BlockSpec((tm, D_MODEL), lambda i: (i, 0)),
                  pl.BlockSpec(w.shape, lambda i: (0, 0), pipeline_mode=pl.Buffered(1))],
        out_specs=pl.BlockSpec((QKV_SLABS, tm, LANE), lambda i: (0, i, 0)),
        compiler_params=pltpu.CompilerParams(dimension_semantics=("arbitrary",), vmem_limit_bytes=VMEM_LIMIT),
        name="qkv_proj",
    )(xn, w)


def _na_kernel(q_ref, k_ref, v_ref, km_ref, vm_ref, bias_ref, o_ref, kcat_ref, vcat_ref, s_ref, *, rows, nblk):
    pad = jnp.zeros((NA_KEYS - NA_WKEYS - N_META, HEAD_DIM), BF16)
    for u in range(ATTN_UNROLL):
        kcat_ref[u, NA_WKEYS:NA_WKEYS + N_META, :] = km_ref[0]
        vcat_ref[u, NA_WKEYS:NA_WKEYS + N_META, :] = vm_ref[0]
        kcat_ref[u, NA_WKEYS + N_META:, :] = pad
        vcat_ref[u, NA_WKEYS + N_META:, :] = pad

    def window_start(i):
        r0 = i * NA_QROWS
        w0 = jnp.minimum(jnp.clip(r0 - NA_WIN_ROWS // 2, 0, rows - NA_WIN_ROWS), rows - NA_WROWS)
        return pl.multiple_of(w0 * GRID_W, GRID_W)

    def scores(i, slot):
        kcat_ref[slot, 0:NA_WKEYS, :] = k_ref[0, pl.ds(window_start(i), NA_WKEYS), :]
        variant = jnp.where(i == 0, 0, jnp.where(i == nblk - 1, 2, 1))
        q = q_ref[0, pl.ds(pl.multiple_of(i * NA_QBLK, NA_QBLK), NA_QBLK), :]
        s_ref[slot] = _scores(q, kcat_ref[slot], bias_ref[variant, 0])

    def finish(i, slot):
        vcat_ref[slot, 0:NA_WKEYS, :] = v_ref[0, pl.ds(window_start(i), NA_WKEYS), :]
        o = _softmax_pv_scores(s_ref[slot], vcat_ref[slot])
        o_ref[0, pl.ds(pl.multiple_of(i * NA_QBLK, NA_QBLK), NA_QBLK), :] = o.astype(BF16)

    _pipelined_blocks(nblk, scores, finish)


def _na_attn(p, pm, bias, b_sz, t):
    rows = t // GRID_W
    nblk = rows // NA_QROWS
    assert rows % NA_QROWS == 0 and nblk >= 3 and rows >= NA_WROWS and nblk % ATTN_UNROLL == 0
    seq = lambda slab: pl.BlockSpec((1, t, HEAD_DIM), lambda b, h: (slab + h, b, 0))
    meta = lambda slab: pl.BlockSpec((1, N_META, HEAD_DIM), lambda b, h: (slab + h, 0, 0))
    return pl.pallas_call(
        functools.partial(_na_kernel, rows=rows, nblk=nblk),
        out_shape=jax.ShapeDtypeStruct((NA_HEADS, b_sz * t, HEAD_DIM), BF16),
        grid=(b_sz, NA_HEADS),
        in_specs=[seq(SLAB_QA), seq(SLAB_KA), seq(SLAB_VA), meta(SLAB_KA), meta(SLAB_VA),
                  pl.BlockSpec((3, 1, NA_QBLK, NA_KEYS), lambda b, h: (0, h, 0, 0))],
        out_specs=pl.BlockSpec((1, t, HEAD_DIM), lambda b, h: (h, b, 0)),
        scratch_shapes=[pltpu.VMEM((ATTN_UNROLL, NA_KEYS, HEAD_DIM), BF16)] * 2
        + [pltpu.VMEM((ATTN_UNROLL, NA_QBLK, NA_KEYS), F32)],
        compiler_params=pltpu.CompilerParams(
            dimension_semantics=("arbitrary", "arbitrary"), vmem_limit_bytes=VMEM_LIMIT),
        name="na_attn",
    )(p, p, p, pm, pm, bias)


def _na_bias(rpb, dtype=F32):
    edge = GRID_W - NA_WIN_COLS
    ext = jnp.pad(rpb.astype(dtype), ((0, 0), (0, 0), (edge, edge + 1)), mode="edge")
    n_r, span = ext.shape[1], 2 * GRID_W
    stream = jnp.broadcast_to(ext[:, :, None, :], (NA_HEADS, n_r, GRID_W, span)).reshape(NA_HEADS, n_r, GRID_W * span)
    skew = stream[:, :, :GRID_W * (span - 1)].reshape(NA_HEADS, n_r, GRID_W, span - 1)
    cols = skew[:, :, :, GRID_W - 1:2 * GRID_W - 1]
    rpad = NA_WROWS - NA_WIN_ROWS
    cols = jnp.pad(cols, ((0, 0), (rpad, rpad), (0, 0), (0, 0)))
    qc = np.arange(GRID_W)[:, None, None]
    j = np.arange(NA_WROWS)[None, :, None]
    kc = np.arange(GRID_W)[None, None, :]
    col_start = np.clip(qc - NA_WIN_COLS // 2, 0, GRID_W - NA_WIN_COLS)
    in_cols = (kc >= col_start) & (kc < col_start + NA_WIN_COLS)
    tables = []
    variants = ([(i, 0) for i in range(NA_QROWS)],
                [(i + NA_WIN_ROWS // 2, i) for i in range(NA_QROWS)],
                [(i + NA_WROWS - NA_QROWS, NA_WROWS - NA_WIN_ROWS) for i in range(NA_QROWS)])
    assert NA_KEYS - NA_WKEYS == GRID_W
    tail = np.where(np.arange(GRID_W) < N_META, 0.0, NEG_INF).astype(np.float32)
    tail = jnp.broadcast_to(jnp.asarray(tail, dtype), (NA_HEADS, GRID_W, 1, GRID_W))
    for variant in variants:
        per_row = []
        for qoff, rs in variant:
            r_lo = (NA_WIN_ROWS - 1) - qoff + rpad
            blk = jnp.transpose(cols[:, r_lo:r_lo + NA_WROWS], (0, 2, 1, 3))
            mask = ((j >= rs) & (j < rs + NA_WIN_ROWS)) & in_cols
            per_row.append(jnp.concatenate([jnp.where(mask[None], blk, NEG_INF), tail], axis=2))
        tables.append(jnp.stack(per_row, axis=1).reshape(NA_HEADS, NA_QBLK, NA_KEYS))
    return jnp.stack(tables)


def _swa_kernel(q_ref, k_ref, v_ref, km_ref, vm_ref, bias_ref, o_ref, kcat_ref, vcat_ref, s_ref, *, t, nb, bps):
    pad = jnp.zeros((SW_KEYS - SW_WKEYS - N_META, HEAD_DIM), BF16)
    for u in range(ATTN_UNROLL):
        kcat_ref[u, SW_WKEYS:SW_WKEYS + N_META, :] = km_ref[0]
        vcat_ref[u, SW_WKEYS:SW_WKEYS + N_META, :] = vm_ref[0]
        kcat_ref[u, SW_WKEYS + N_META:, :] = pad
        vcat_ref[u, SW_WKEYS + N_META:, :] = pad
    step = pl.program_id(2)

    def window_start(i):
        n = step * bps + i
        return pl.multiple_of(jnp.clip((n - 1) * SW_BLOCK, 0, t - SW_WKEYS), SW_BLOCK)

    def scores(i, slot):
        n = step * bps + i
        kcat_ref[slot, 0:SW_WKEYS, :] = k_ref[0, pl.ds(window_start(i), SW_WKEYS), :]
        variant = jnp.where(n == 0, 0, jnp.where(n == nb - 1, 2, 1))
        q = q_ref[:, pl.ds(pl.multiple_of(i * SW_BLOCK, SW_BLOCK), SW_BLOCK), :]
        bias = bias_ref[variant].reshape(SW_GROUP * SW_BLOCK, SW_KEYS)
        s_ref[slot] = _scores(q.reshape(SW_GROUP * SW_BLOCK, HEAD_DIM), kcat_ref[slot], bias)

    def finish(i, slot):
        vcat_ref[slot, 0:SW_WKEYS, :] = v_ref[0, pl.ds(window_start(i), SW_WKEYS), :]
        o = _softmax_pv_scores(s_ref[slot], vcat_ref[slot])
        qs = pl.multiple_of(i * SW_BLOCK, SW_BLOCK)
        o_ref[:, pl.ds(qs, SW_BLOCK), :] = o.reshape(SW_GROUP, SW_BLOCK, HEAD_DIM).astype(BF16)

    _pipelined_blocks(bps, scores, finish)


def _swa_attn(p, pm, bias, b_sz, t, tq):
    nb = t // SW_BLOCK
    assert nb >= 3 and t % tq == 0 and (tq // SW_BLOCK) % ATTN_UNROLL == 0
    steps = t // tq
    bps = tq // SW_BLOCK
    kv = lambda slab: pl.BlockSpec((1, t, HEAD_DIM), lambda b, g, s: (slab + g, b, 0))
    meta = lambda slab: pl.BlockSpec((1, N_META, HEAD_DIM), lambda b, g, s: (slab + g, 0, 0))
    return pl.pallas_call(
        functools.partial(_swa_kernel, t=t, nb=nb, bps=bps),
        out_shape=jax.ShapeDtypeStruct((SW_Q_HEADS, b_sz * t, HEAD_DIM), BF16),
        grid=(b_sz, SW_KV_HEADS, steps),
        in_specs=[pl.BlockSpec((SW_GROUP, tq, HEAD_DIM), lambda b, g, s: (SLAB_QB // SW_GROUP + g, b * steps + s, 0)),
                  kv(SLAB_KB), kv(SLAB_VB), meta(SLAB_KB), meta(SLAB_VB),
                  pl.BlockSpec((3, SW_GROUP, SW_BLOCK, SW_KEYS), lambda b, g, s: (0, g, 0, 0))],
        out_specs=pl.BlockSpec((SW_GROUP, tq, HEAD_DIM), lambda b, g, s: (g, b * steps + s, 0)),
        scratch_shapes=[pltpu.VMEM((ATTN_UNROLL, SW_KEYS, HEAD_DIM), BF16)] * 2
        + [pltpu.VMEM((ATTN_UNROLL, SW_GROUP * SW_BLOCK, SW_KEYS), F32)],
        compiler_params=pltpu.CompilerParams(
            dimension_semantics=("arbitrary", "arbitrary", "arbitrary"), vmem_limit_bytes=VMEM_LIMIT),
        name="swa_attn",
    )(p, p, p, pm, pm, bias)


def _sw_slopes():
    return np.power(2.0, -8.0 * np.arange(1, SW_Q_HEADS + 1, dtype=np.float64) / SW_Q_HEADS).astype(np.float32)


def _sw_bias(sink, dtype=F32):
    i = np.arange(SW_BLOCK)[:, None]
    j = np.arange(SW_WKEYS)[None, :]
    slopes = jnp.asarray(_sw_slopes(), dtype)[:, None, None]
    tables = []
    for qo in (0, SW_BLOCK, 2 * SW_BLOCK):
        dist = np.abs(qo + i - j)
        pen = -(jnp.asarray(dist, dtype)[None] * slopes)
        tables.append(jnp.where((dist <= SW_WINDOW)[None], pen, NEG_INF))
    tab = jnp.stack(tables)
    lead = tab.shape[:-1]
    snk = jnp.broadcast_to(sink.astype(dtype)[None, :, None, None], lead + (1,))
    return jnp.concatenate([tab, jnp.zeros(lead + (N_META,), dtype), snk,
                            jnp.full(lead + (SW_KEYS - SW_WKEYS - N_META - 1,), NEG_INF, dtype)], axis=-1)


def _sw_meta_bias(sink, dtype=F32):
    q = np.arange(N_META)[:, None]
    k = np.arange(SW_BLOCK)[None, :]
    dist = (N_META + k) - q
    slopes = jnp.asarray(_sw_slopes(), dtype)[:, None, None]
    pen = jnp.where((dist <= SW_WINDOW)[None], -(slopes * jnp.asarray(dist, dtype)[None]), NEG_INF)
    lead = pen.shape[:-1]
    snk = jnp.broadcast_to(sink.astype(dtype)[:, None, None], lead + (1,))
    return jnp.concatenate([jnp.zeros(lead + (N_META,), dtype), pen, snk,
                            jnp.full(lead + (SW_MKEYS - N_META - SW_BLOCK - 1,), NEG_INF, dtype)], axis=-1)


def _meta_attn_kernel(pm_ref, k0_ref, v0_ref, bias_ref, oa_ref, ob_ref, kcat_ref, vcat_ref):
    for h in range(NA_HEADS):
        o = _softmax_pv(pm_ref[SLAB_QA + h], pm_ref[SLAB_KA + h], pm_ref[SLAB_VA + h], None)
        oa_ref[h] = o.astype(BF16)
    pad = jnp.zeros((SW_MKEYS - N_META - SW_BLOCK, HEAD_DIM), BF16)
    for g in range(SW_KV_HEADS):
        kcat_ref[0:N_META, :] = pm_ref[SLAB_KB + g]
        vcat_ref[0:N_META, :] = pm_ref[SLAB_VB + g]
        kcat_ref[N_META:N_META + SW_BLOCK, :] = k0_ref[g]
        vcat_ref[N_META:N_META + SW_BLOCK, :] = v0_ref[g]
        kcat_ref[N_META + SW_BLOCK:, :] = pad
        vcat_ref[N_META + SW_BLOCK:, :] = pad
        q = jnp.concatenate([pm_ref[SLAB_QB + g * SW_GROUP + r] for r in range(SW_GROUP)], axis=0)
        bias = bias_ref[g * SW_GROUP:(g + 1) * SW_GROUP].reshape(SW_GROUP * N_META, SW_MKEYS)
        o = _softmax_pv(q, kcat_ref[...], vcat_ref[...], bias)
        for r in range(SW_GROUP):
            ob_ref[g * SW_GROUP + r] = o[r * N_META:(r + 1) * N_META].astype(BF16)


def _meta_attn(p, pm, bias, b_sz, t):
    blocks_per_seq = t // SW_BLOCK
    first_block = lambda slab: pl.BlockSpec((SW_KV_HEADS, SW_BLOCK, HEAD_DIM),
                                            lambda b: (slab // SW_KV_HEADS, b * blocks_per_seq, 0))
    out = jax.ShapeDtypeStruct((NA_HEADS, b_sz * N_META, HEAD_DIM), BF16)
    return pl.pallas_call(
        _meta_attn_kernel,
        out_shape=(out, out),
        grid=(b_sz,),
        in_specs=[pl.BlockSpec((QKV_SLABS, N_META, HEAD_DIM), lambda b: (0, 0, 0)),
                  first_block(SLAB_KB), first_block(SLAB_VB),
                  pl.BlockSpec((SW_Q_HEADS, N_META, SW_MKEYS), lambda b: (0, 0, 0))],
        out_specs=(pl.BlockSpec((NA_HEADS, N_META, HEAD_DIM), lambda b: (0, b, 0)),
                   pl.BlockSpec((SW_Q_HEADS, N_META, HEAD_DIM), lambda b: (0, b, 0))),
        scratch_shapes=[pltpu.VMEM((SW_MKEYS, HEAD_DIM), BF16), pltpu.VMEM((SW_MKEYS, HEAD_DIM), BF16)],
        compiler_params=pltpu.CompilerParams(dimension_semantics=("arbitrary",)),
        name="meta_attn",
    )(pm, p, p, bias)


def _post_kernel(x_ref, lg_ref, lb_ref, oa_ref, ob_ref, ga_ref, gb_ref, wna_ref, wsw_ref, wout_ref,
                 g1_ref, b1_ref, h_ref, *, parts):
    hr = x_ref.shape[0] // parts
    for r in range(parts):
        rows = slice(r * hr, (r + 1) * hr)
        oa = jnp.concatenate([oa_ref[h, rows, :] for h in range(NA_HEADS)], axis=1)
        ob = jnp.concatenate([ob_ref[h, rows, :] for h in range(SW_Q_HEADS)], axis=1)
        merged = []
        for c0 in range(0, D_MODEL, PROJ_CHUNK):
            cols = slice(c0, c0 + PROJ_CHUNK)
            slabs = range(c0 // LANE, (c0 + PROJ_CHUNK) // LANE)
            a = jnp.dot(oa, wna_ref[:, cols], preferred_element_type=F32)
            b = jnp.dot(ob, wsw_ref[:, cols], preferred_element_type=F32)
            ga = jnp.concatenate([ga_ref[s, rows, :] for s in slabs], axis=1).astype(F32)
            gb = jnp.concatenate([gb_ref[s, rows, :] for s in slabs], axis=1).astype(F32)
            merged.append((jax.nn.sigmoid(ga) * a + jax.nn.sigmoid(gb) * b).astype(BF16))
        merged = jnp.concatenate(merged, axis=1)
        y = jnp.dot(merged, wout_ref[...], preferred_element_type=F32)
        h0 = _ln_rows(x_ref[rows, :], lg_ref[...], lb_ref[...])
        h_ref[rows, :] = _ln_rows(ALPHA * h0 + y, g1_ref[...], b1_ref[...])


def _post_attn(x2, lg, lb, oa, ob, gates, wna, wsw, wout, g1, b1, tm, parts):
    m = x2.shape[0]
    n_g = D_MODEL // LANE
    const = lambda shape: pl.BlockSpec(shape, lambda i: (0,) * len(shape), pipeline_mode=pl.Buffered(1))
    heads = pl.BlockSpec((NA_HEADS, tm, HEAD_DIM), lambda i: (0, i, 0))
    return pl.pallas_call(
        functools.partial(_post_kernel, parts=parts),
        out_shape=jax.ShapeDtypeStruct((m, D_MODEL), F32),
        grid=(m // tm,),
        in_specs=[pl.BlockSpec((tm, D_MODEL), lambda i: (i, 0)),
                  const((1, D_MODEL)), const((1, D_MODEL)),
                  heads, heads,
                  pl.BlockSpec((n_g, tm, LANE), lambda i: (0, i, 0)),
                  pl.BlockSpec((n_g, tm, LANE), lambda i: (1, i, 0)),
                  const(wna.shape), const(wsw.shape), const(wout.shape),
                  const((1, D_MODEL)), const((1, D_MODEL))],
        out_specs=pl.BlockSpec((tm, D_MODEL), lambda i: (i, 0)),
        compiler_params=pltpu.CompilerParams(dimension_semantics=("arbitrary",), vmem_limit_bytes=VMEM_LIMIT),
        name="post_attn",
    )(x2, lg, lb, oa, ob, gates, gates, wna, wsw, wout, g1, b1)


HALO = BF16_ROWS
FFN_SUB = MXU_COLS


_GELU_C = 2.0 * float(np.sqrt(2.0 / np.pi)) * float(np.log2(np.e))


def _gelu_tanh(x):
    u = x * (x * x * (-_GELU_C * 0.044715) - _GELU_C)
    return x / (1.0 + jnp.exp2(u))


def _ffn_kernel(h_ref, hn_ref, hm_ref, wg_ref, wv_ref, cwb_ref, wd_ref, ln_ref, o_ref, hb_ref, act_a, act_b,
                *, tm, tf, tiles_per_seq):
    i = pl.program_id(0)
    f = pl.program_id(1)
    nf = pl.num_programs(1) - 1
    pos = i % tiles_per_seq

    @pl.when(jnp.logical_and(f == 0, pos == 0))
    def _():
        hb_ref[0:HALO, :] = hm_ref[...].astype(BF16)

    @pl.when(jnp.logical_and(f == 0, pos != 0))
    def _():
        hb_ref[0:HALO, :] = hb_ref[tm:tm + HALO, :]

    @pl.when(f == 0)
    def _():
        nxt = jnp.where(pos == tiles_per_seq - 1, 0.0, hn_ref[...])
        hb_ref[HALO:HALO + tm, :] = h_ref[...].astype(BF16)
        hb_ref[HALO + tm:, :] = nxt.astype(BF16)
        o_ref[...] = jnp.zeros_like(o_ref)

    n = tm + 2 * HALO

    def up_stage(act_out):
        for c0 in range(0, tf, FFN_SUB):
            sl = slice(c0, c0 + FFN_SUB)
            gp = jnp.concatenate([jnp.dot(hb_ref[r0:r0 + n // 2, :], wg_ref[:, sl], preferred_element_type=F32)
                                  for r0 in (0, n // 2)], axis=0)
            vl = jnp.concatenate([jnp.dot(hb_ref[HALO + r0:HALO + r0 + tm // 2, :], wv_ref[:, sl],
                                          preferred_element_type=F32) for r0 in (0, tm // 2)], axis=0)
            up = pltpu.roll(gp, 1, 0)
            dn = pltpu.roll(gp, n - 1, 0)
            gate = (up * cwb_ref[0:1, sl] + gp * cwb_ref[1:2, sl] + dn * cwb_ref[2:3, sl])[HALO:HALO + tm] \
                + cwb_ref[3:4, sl]
            act_out[:, sl] = (_gelu_tanh(gate) * vl).astype(BF16)

    def down(act_in):
        return jnp.concatenate([jnp.dot(act_in[r0:r0 + tm // 2, :], wd_ref[...], preferred_element_type=F32)
                                for r0 in (0, tm // 2)], axis=0)

    @pl.when(f == 0)
    def _():
        up_stage(act_a)

    @pl.when(jnp.logical_and(jnp.logical_and(f > 0, f < nf), f % 2 == 1))
    def _():
        up_stage(act_b)
        o_ref[...] += down(act_a)

    @pl.when(jnp.logical_and(jnp.logical_and(f > 0, f < nf), f % 2 == 0))
    def _():
        up_stage(act_a)
        o_ref[...] += down(act_b)

    @pl.when(f == nf)
    def _():
        last = act_a if (D_FF // tf - 1) % 2 == 0 else act_b
        o_ref[...] = _ln_rows(ALPHA * h_ref[...] + (o_ref[...] + down(last)), ln_ref[0:1, :], ln_ref[1:2, :])


def _ffn(h, h_meta, w_in, cwb, wd, ln2, t, tm, tf):
    m = h.shape[0]
    nf = D_FF // tf
    tiles_per_seq = t // tm
    hb = tm // HALO
    last_hb = m // HALO - 1
    up_blk = lambda f: jnp.minimum(f, nf - 1)
    return pl.pallas_call(
        functools.partial(_ffn_kernel, tm=tm, tf=tf, tiles_per_seq=tiles_per_seq),
        out_shape=jax.ShapeDtypeStruct((m, D_MODEL), F32),
        grid=(m // tm, nf + 1),
        in_specs=[pl.BlockSpec((tm, D_MODEL), lambda i, f: (i, 0)),
                  pl.BlockSpec((HALO, D_MODEL), lambda i, f: (jnp.minimum((i + 1) * hb, last_hb), 0)),
                  pl.BlockSpec((N_META, D_MODEL), lambda i, f: (i // tiles_per_seq, 0)),
                  pl.BlockSpec((D_MODEL, tf), lambda i, f: (0, up_blk(f))),
                  pl.BlockSpec((D_MODEL, tf), lambda i, f: (0, up_blk(f) + nf)),
                  pl.BlockSpec((4, tf), lambda i, f: (0, up_blk(f))),
                  pl.BlockSpec((tf, D_MODEL), lambda i, f: (jnp.maximum(f - 1, 0), 0)),
                  pl.BlockSpec((2, D_MODEL), lambda i, f: (0, 0))],
        out_specs=pl.BlockSpec((tm, D_MODEL), lambda i, f: (i, 0)),
        scratch_shapes=[pltpu.VMEM((tm + 2 * HALO, D_MODEL), BF16),
                        pltpu.VMEM((tm, tf), BF16), pltpu.VMEM((tm, tf), BF16)],
        compiler_params=pltpu.CompilerParams(
            dimension_semantics=("arbitrary", "arbitrary"), vmem_limit_bytes=FFN_VMEM_LIMIT),
        name="ffn",
    )(h, h, h_meta, w_in, w_in, cwb, wd, ln2)


def _layer_weights(w_in, w_proj_na, w_proj_sw, w_out, w_ffn_in, w_ffn_down):
    return (w_in[:, QKV_COLS:].astype(BF16), w_in[:, :QKV_COLS].astype(BF16),
            w_proj_na.astype(BF16), w_proj_sw.astype(BF16), w_out.astype(BF16),
            w_ffn_in.astype(BF16), w_ffn_down.astype(BF16))


def _trunk(x, meta_tokens, pm, gm, lg, lb, wts, tables, g1, b1, cwb, ln2):
    b_sz, t, _ = x.shape
    wgate, wqkv, wna, wsw, wout, wffn, wdown = wts
    na_bias, sw_bias, swm_bias = tables
    x2 = x.reshape(b_sz * t, D_MODEL)
    xn, gates = _ln_gates(x2, lg, lb, wgate, tm=512, parts=2)
    p = _qkv_proj(xn, wqkv, tm=1024)
    oa = _na_attn(p, pm, na_bias, b_sz, t)
    ob = _swa_attn(p, pm, sw_bias, b_sz, t, tq=2048)
    oa_m, ob_m = _meta_attn(p, pm, swm_bias, b_sz, t)
    h = _post_attn(x2, lg, lb, oa, ob, gates, wna, wsw, wout, g1, b1, tm=512, parts=2)
    x_m = jnp.tile(meta_tokens, (b_sz, 1))
    gates_m = jnp.tile(gm, (1, b_sz, 1))
    h_m = _post_attn(x_m, lg, lb, oa_m, ob_m, gates_m, wna, wsw, wout, g1, b1, tm=b_sz * N_META, parts=1)
    y = _ffn(h, h_m, wffn, cwb, wdown, ln2, t, tm=1024, tf=512)
    return y.reshape(b_sz, t, D_MODEL)


def kernel(x_prompt, x_sample, meta_tokens, ln_emb_g, ln_emb_b, w_in, na_rpb, sw_sink, w_proj_na, w_proj_sw, w_out, ln1_g, ln1_b, w_ffn_in, ffn_conv_w, ffn_conv_b, w_ffn_down, ln2_g, ln2_b):
    assert DEPTH == 1 and w_in.shape[0] == DEPTH
    row = lambda v: v.reshape(1, -1)
    lg, lb = row(ln_emb_g), row(ln_emb_b)
    wts = _layer_weights(w_in[0], w_proj_na[0], w_proj_sw[0], w_out[0], w_ffn_in[0], w_ffn_down[0])
    tables = (_na_bias(na_rpb[0]), _sw_bias(sw_sink[0]), _sw_meta_bias(sw_sink[0]))
    xn_m, gm = _ln_gates(meta_tokens, lg, lb, wts[0], tm=N_META, parts=1)
    pm = _qkv_proj(xn_m, wts[1], tm=N_META)
    cwb = jnp.concatenate([ffn_conv_w[0], row(ffn_conv_b[0])], axis=0)
    ln2 = jnp.stack([ln2_g[0], ln2_b[0]])
    args = (meta_tokens, pm, gm, lg, lb, wts, tables, row(ln1_g[0]), row(ln1_b[0]), cwb, ln2)
    return (_trunk(x_prompt, *args), _trunk(x_sample, *args))
```
